```python
import math
import jax, jax.numpy as jnp
from jax import lax
import numpy as np

D_MODEL = 1024
BATCH = 16
SEQ = 2048
DEPTH = 4
DEC_BATCH = 2
DEC_SEQ = 16384
PAST_LEN = 128

DN_ALPHA = (2.0 * DEPTH) ** 0.25
DN_BETA = (8.0 * DEPTH) ** -0.25
LN_EPS = 1e-5
RMS_EPS = 1e-6
N_EVEN = (DEPTH + 1) // 2
N_ODD = DEPTH // 2

D_FF = 2816

MLA_HEADS = 8
MLA_NOPE = 64
MLA_ROPE = 32
MLA_QK = MLA_NOPE + MLA_ROPE
MLA_V = 64
Q_LORA = 384
KV_LORA = 256
ROPE_THETA = 10000.0
Q_BLOCK = 128
MLA_SCALE = MLA_QK ** -0.5
MLA_IN = Q_LORA + KV_LORA + MLA_ROPE

RW_HEADS = 8
RW_HEAD = 64
RW_WIDTH = RW_HEADS * RW_HEAD
DECAY_LORA = 64
AAA_LORA = 64
GATE_LORA = 128
RW_GN_EPS = 64e-5
RW_IN = 3 * RW_WIDTH + 2 * DECAY_LORA + AAA_LORA + GATE_LORA

EVEN_IN = MLA_IN + RW_IN
MIX_EVEN = MLA_HEADS * MLA_V + RW_WIDTH

M_INNER = 2 * D_MODEL
M_HEADDIM = 64
M_HEADS = M_INNER // M_HEADDIM
M_GROUPS = 4
M_HPG = M_HEADS // M_GROUPS
M_STATE = 128
M_CONV = 5
CHUNK = 128
M_CONV_DIM = M_INNER + 2 * M_GROUPS * M_STATE
ODD_IN = 2 * M_INNER + 2 * M_GROUPS * M_STATE + 2 * M_HEADS

kernel_name = "hybrid_mla_rwkv7_mamba2_deepnorm_encoder"


def layer_norm(x, g, b):
    xf = x.astype(jnp.float32)
    mu = jnp.mean(xf, -1, keepdims=True)
    var = jnp.mean(jnp.square(xf - mu), -1, keepdims=True)
    return ((xf - mu) * lax.rsqrt(var + LN_EPS) * g + b).astype(x.dtype)


def rms_norm(x, g):
    xf = x.astype(jnp.float32)
    y = xf * lax.rsqrt(jnp.mean(xf * xf, -1, keepdims=True) + RMS_EPS)
    return (y * g).astype(x.dtype)


def swiglu(x, w_in, w_out):
    gate, up = jnp.split(x @ w_in, 2, axis=-1)
    return (jax.nn.silu(gate) * up) @ w_out


def centred_shift(p):
    prev = jnp.pad(p[:, :-1], ((0, 0), (1, 0), (0, 0)))
    nxt = jnp.pad(p[:, 1:], ((0, 0), (0, 1), (0, 0)))
    return 0.5 * (prev + nxt)


def rope_tables(seq, dtype):
    half = MLA_ROPE // 2
    inv = ROPE_THETA ** (-jnp.arange(half, dtype=jnp.float32) / half)
    ang = jnp.arange(seq, dtype=jnp.float32)[:, None] * inv[None, :]
    return jnp.cos(ang).astype(dtype), jnp.sin(ang).astype(dtype)


def apply_rope(x, cos, sin):
    x1, x2 = jnp.split(x, 2, axis=-1)
    return jnp.concatenate([x1 * cos - x2 * sin, x1 * sin + x2 * cos], axis=-1)


def mla(cols, q_norm_g, w_uq, kv_norm_g, w_ukv):
    B_, S_, _ = cols.shape
    cq, ckv, kr = jnp.split(cols, [Q_LORA, Q_LORA + KV_LORA], axis=-1)
    q = (rms_norm(cq, q_norm_g) @ w_uq).reshape(B_, S_, MLA_HEADS, MLA_QK)
    qn, qr = q[..., :MLA_NOPE], q[..., MLA_NOPE:]
    kv = (rms_norm(ckv, kv_norm_g) @ w_ukv).reshape(B_, S_, MLA_HEADS, MLA_NOPE + MLA_V)
    kn, v = kv[..., :MLA_NOPE], kv[..., MLA_NOPE:]
    cos, sin = rope_tables(S_, cols.dtype)
    qr = apply_rope(qr, cos[:, None, :], sin[:, None, :])
    kr = apply_rope(kr, cos, sin)
    nb = S_ // Q_BLOCK

    def to_blocks(t):
        return jnp.moveaxis(t.reshape((B_, nb, Q_BLOCK) + t.shape[2:]), 1, 0)

    def block(qs):
        qn_b, qr_b = qs
        s = (jnp.einsum('bqhd,bkhd->bhqk', qn_b, kn)
             + jnp.einsum('bqhd,bkd->bhqk', qr_b, kr))
        pr = jax.nn.softmax(s.astype(jnp.float32) * MLA_SCALE, axis=-1).astype(v.dtype)
        return jnp.einsum('bhqk,bkhd->bqhd', pr, v)

    o = lax.map(block, (to_blocks(qn), to_blocks(qr)))
    return jnp.moveaxis(o, 0, 1).reshape(B_, S_, MLA_HEADS * MLA_V)


def wkv_scan(r, w, k, v, kk, kka, reverse):
    B_, _, H, N = r.shape
    xs = tuple(jnp.swapaxes(t, 0, 1) for t in (r, w, k, v, kk, kka))

    def step(S, inp):
        r_t, w_t, k_t, v_t, kk_t, kka_t = inp
        s_kk = jnp.einsum('bhvk,bhk->bhv', S, kk_t)
        S = (S * w_t[:, :, None, :] - s_kk[..., None] * kka_t[:, :, None, :]
             + v_t[..., None] * k_t[:, :, None, :])
        return S, jnp.einsum('bhvk,bhk->bhv', S, r_t)

    S0 = jnp.zeros((B_, H, N, N), jnp.float32)
    _, ys = lax.scan(step, S0, xs, reverse=reverse)
    return jnp.swapaxes(ys, 0, 1)


def rwkv7(cols, mu, w0, w2, a0, a2, g2, k_k, k_a, r_k, gn_g, gn_b):
    B_, S_, _ = cols.shape
    cols = cols + (centred_shift(cols) - cols) * mu
    splits = [RW_WIDTH, 2 * RW_WIDTH, 3 * RW_WIDTH, 3 * RW_WIDTH + DECAY_LORA,
              3 * RW_WIDTH + 2 * DECAY_LORA, 3 * RW_WIDTH + 2 * DECAY_LORA + AAA_LORA]
    r, k, v, pwf, pwb, pa, pg = jnp.split(cols, splits, axis=-1)
    a = jax.nn.sigmoid(a0 + pa @ a2)
    g = jax.nn.sigmoid(pg) @ g2

    def heads(t):
        return t.reshape(B_, S_, RW_HEADS, RW_HEAD)

    kkf = heads(k * k_k).astype(jnp.float32)
    kk = (kkf * lax.rsqrt(jnp.sum(kkf * kkf, -1, keepdims=True) + 1e-12)).astype(cols.dtype)
    k = k * (1.0 + (a - 1.0) * k_a)
    r_h, k_h, v_h, a_h = heads(r), heads(k), heads(v), heads(a)

    def decay(pw, w0_d, w2_d):
        w = -jax.nn.softplus(-(w0_d + jnp.tanh(pw) @ w2_d)) - 0.5
        return heads(jnp.exp(-jnp.exp(w.astype(jnp.float32))))

    kka = kk * a_h
    y = (wkv_scan(r_h, decay(pwf, w0[0], w2[0]), k_h, v_h, kk, kka, False)
         + wkv_scan(r_h, decay(pwb, w0[1], w2[1]), k_h, v_h, kk, kka, True))
    mean = jnp.mean(y, -1, keepdims=True)
    var = jnp.mean(jnp.square(y - mean), -1, keepdims=True)
    yn = ((y - mean) * lax.rsqrt(var + RW_GN_EPS)).reshape(B_, S_, RW_WIDTH) * gn_g + gn_b
    bonus = (jnp.sum(r_h * k_h * r_k, -1, keepdims=True) * v_h).reshape(B_, S_, RW_WIDTH)
    return (yn.astype(cols.dtype) + bonus) * g


def even_mixer(x, j, P):
    cols = x @ P['w_in_even'][j]
    o_a = mla(cols[..., :MLA_IN], P['mla_q_norm'][j], P['mla_w_uq'][j],
              P['mla_kv_norm'][j], P['mla_w_ukv'][j])
    o_b = rwkv7(cols[..., MLA_IN:], P['rw_mu'][j], P['rw_w0'][j], P['rw_w2'][j],
                P['rw_a0'][j], P['rw_a2'][j], P['rw_g2'][j], P['rw_k_k'][j],
                P['rw_k_a'][j], P['rw_r_k'][j], P['rw_gn_g'][j], P['rw_gn_b'][j])
    return jnp.concatenate([o_a, o_b], axis=-1) @ P['w_out_even'][j]


def depthwise_conv(u, w, b):
    out = lax.conv_general_dilated(
        u, w[:, None, :].astype(u.dtype), window_strides=(1,),
        padding=[(M_CONV // 2, M_CONV // 2)],
        dimension_numbers=('NWC', 'WIO', 'NWC'), feature_group_count=u.shape[-1])
    return out + b


def segsum(a):
    T = a.shape[-1]
    cs = jnp.cumsum(a, axis=-1)
    diff = cs[..., :, None] - cs[..., None, :]
    return jnp.where(jnp.tril(jnp.ones((T, T), dtype=bool)), diff, -jnp.inf)


def ssd(x, dt, A, Bm, Cm):
    B_, S_, G, R, P = x.shape
    c = S_ // CHUNK
    xd = (x * dt[..., None]).reshape(B_, c, CHUNK, G, R, P)
    a = jnp.moveaxis((dt * A).reshape(B_, c, CHUNK, G, R), 2, -1)
    a_cs = jnp.cumsum(a, axis=-1)
    Bc = Bm.reshape(B_, c, CHUNK, G, M_STATE).astype(jnp.float32)
    Cc = Cm.reshape(B_, c, CHUNK, G, M_STATE).astype(jnp.float32)
    Lmat = jnp.exp(segsum(a))
    CB = jnp.einsum('bclgn,bcsgn->bcgls', Cc, Bc)
    y_diag = jnp.einsum('bcgls,bcgrls,bcsgrp->bclgrp', CB, Lmat, xd)
    decay_states = jnp.exp(a_cs[..., -1:] - a_cs)
    states = jnp.einsum('bclgn,bcgrl,bclgrp->bcgrpn', Bc, decay_states, xd)
    states = jnp.pad(states, ((0, 0), (1, 0), (0, 0), (0, 0), (0, 0), (0, 0)))
    a_chunk = jnp.moveaxis(jnp.pad(a_cs[..., -1], ((0, 0), (1, 0), (0, 0), (0, 0))), 1, -1)
    decay_chunk = jnp.exp(segsum(a_chunk))
    states = jnp.einsum('bgrzc,bcgrpn->bzgrpn', decay_chunk, states)[:, :-1]
    y_off = jnp.einsum('bclgn,bcgrpn,bcgrl->bclgrp', Cc, states, jnp.exp(a_cs))
    return (y_diag + y_off).reshape(B_, S_, G, R, P)


def mamba2(x, j, P):
    B_, S_, _ = x.shape
    cols = x @ P['w_in_odd'][j]
    z, xbc, dt = jnp.split(cols, [M_INNER, M_INNER + M_CONV_DIM], axis=-1)
    xbc = jax.nn.silu(depthwise_conv(xbc, P['m_conv_w'][j], P['m_conv_b'][j]))
    xs, Bm, Cm = jnp.split(xbc, [M_INNER, M_INNER + M_GROUPS * M_STATE], axis=-1)
    xs = xs.reshape(B_, S_, M_GROUPS, M_HPG, M_HEADDIM)
    Bm = Bm.reshape(B_, S_, M_GROUPS, M_STATE)
    Cm = Cm.reshape(B_, S_, M_GROUPS, M_STATE)
    dt = jax.nn.softplus(dt.astype(jnp.float32)
                         + P['m_dt_bias'][j].reshape(2 * M_HEADS).astype(jnp.float32))
    dt = dt.reshape(B_, S_, 2, M_GROUPS, M_HPG)
    A = -jnp.exp(P['m_A_log'][j].astype(jnp.float32)).reshape(2, M_GROUPS, M_HPG)

    def flip(t):
        return jnp.flip(t, axis=1)

    y = (ssd(xs, dt[:, :, 0], A[0], Bm, Cm)
         + flip(ssd(flip(xs), flip(dt[:, :, 1]), A[1], flip(Bm), flip(Cm)))
         + xs * P['m_D'][j].reshape(M_GROUPS, M_HPG, 1))
    y = y.reshape(B_, S_, M_INNER) * jax.nn.silu(z)
    y = rms_norm(y.reshape(B_, S_, M_GROUPS, M_INNER // M_GROUPS),
                 P['m_norm_g'][j].reshape(M_GROUPS, M_INNER // M_GROUPS)).reshape(B_, S_, M_INNER)
    return y.astype(x.dtype) @ P['w_out_odd'][j]


def trunk(x, P):
    for i in range(DEPTH):
        x = layer_norm(DN_ALPHA * x + 0.5 * swiglu(x, P['ffn1_in'][i], P['ffn1_out'][i]),
                       P['ln_g'][i, 0], P['ln_b'][i, 0])
        m = even_mixer(x, i // 2, P) if i % 2 == 0 else mamba2(x, i // 2, P)
        x = layer_norm(DN_ALPHA * x + m, P['ln_g'][i, 1], P['ln_b'][i, 1])
        x = layer_norm(DN_ALPHA * x + 0.5 * swiglu(x, P['ffn2_in'][i], P['ffn2_out'][i]),
                       P['ln_g'][i, 2], P['ln_b'][i, 2])
    return x


def setup_inputs(seed: int = 0) -> dict:
    key = jax.random.key(seed)
    ks = jax.random.split(key, 40)
    f32 = jnp.float32

    def nrm(i, shape, scale):
        return scale * jax.random.normal(ks[i], shape, f32)

    def unif(i, shape, lo, hi):
        return jax.random.uniform(ks[i], shape, f32, lo, hi)

    D = D_MODEL
    dt0 = jnp.exp(unif(28, (N_ODD, 2, M_HEADS), math.log(1e-3), math.log(1e-1)))
    return {
        'x_prompt': nrm(0, (BATCH, SEQ, D), 1.0),
        'x_sample': nrm(1, (DEC_BATCH, DEC_SEQ, D), 1.0),
        'ffn1_in': nrm(2, (DEPTH, D, 2 * D_FF), D ** -0.5),
        'ffn1_out': nrm(3, (DEPTH, D_FF, D), DN_BETA * D_FF ** -0.5),
        'ffn2_in': nrm(4, (DEPTH, D, 2 * D_FF), D ** -0.5),
        'ffn2_out': nrm(5, (DEPTH, D_FF, D), DN_BETA * D_FF ** -0.5),
        'ln_g': 1.0 + nrm(6, (DEPTH, 3, D), 0.02),
        'ln_b': nrm(7, (DEPTH, 3, D), 0.02),
        'w_in_even': nrm(8, (N_EVEN, D, EVEN_IN), D ** -0.5),
        'w_out_even': nrm(9, (N_EVEN, MIX_EVEN, D), DN_BETA * MIX_EVEN ** -0.5),
        'mla_q_norm': 1.0 + nrm(10, (N_EVEN, Q_LORA), 0.02),
        'mla_w_uq': nrm(11, (N_EVEN, Q_LORA, MLA_HEADS * MLA_QK), Q_LORA ** -0.5),
        'mla_kv_norm': 1.0 + nrm(12, (N_EVEN, KV_LORA), 0.02),
        'mla_w_ukv': nrm(13, (N_EVEN, KV_LORA, MLA_HEADS * (MLA_NOPE + MLA_V)), KV_LORA ** -0.5),
        'rw_mu': unif(14, (N_EVEN, RW_IN), 0.0, 1.0),
        'rw_w0': unif(15, (N_EVEN, 2, RW_WIDTH), -6.0, -1.0),
        'rw_w2': nrm(16, (N_EVEN, 2, DECAY_LORA, RW_WIDTH), 0.1 * DECAY_LORA ** -0.5),
        'rw_a0': nrm(17, (N_EVEN, RW_WIDTH), 0.1),
        'rw_a2': nrm(18, (N_EVEN, AAA_LORA, RW_WIDTH), 0.1 * AAA_LORA ** -0.5),
        'rw_g2': nrm(19, (N_EVEN, GATE_LORA, RW_WIDTH), GATE_LORA ** -0.5),
        'rw_k_k': 0.85 + nrm(20, (N_EVEN, RW_WIDTH), 0.02),
        'rw_k_a': 1.0 + nrm(21, (N_EVEN, RW_WIDTH), 0.02),
        'rw_r_k': nrm(22, (N_EVEN, RW_HEADS, RW_HEAD), 0.1),
        'rw_gn_g': 1.0 + nrm(23, (N_EVEN, RW_WIDTH), 0.02),
        'rw_gn_b': nrm(24, (N_EVEN, RW_WIDTH), 0.02),
        'w_in_odd': nrm(25, (N_ODD, D, ODD_IN), D ** -0.5),
        'm_conv_w': nrm(26, (N_ODD, M_CONV, M_CONV_DIM), M_CONV ** -0.5),
        'm_conv_b': nrm(27, (N_ODD, M_CONV_DIM), 0.02),
        'm_dt_bias': dt0 + jnp.log(-jnp.expm1(-dt0)),
        'm_A_log': jnp.log(unif(29, (N_ODD, 2, M_HEADS), 1.0, 16.0)),
        'm_D': 1.0 + nrm(30, (N_ODD, M_HEADS), 0.02),
        'm_norm_g': 1.0 + nrm(31, (N_ODD, M_INNER), 0.02),
        'w_out_odd': nrm(32, (N_ODD, M_INNER, D), DN_BETA * M_INNER ** -0.5),
    }


def reference(x_prompt, x_sample, ffn1_in, ffn1_out, ffn2_in, ffn2_out, ln_g, ln_b,
              w_in_even, w_out_even, mla_q_norm, mla_w_uq, mla_kv_norm, mla_w_ukv,
              rw_mu, rw_w0, rw_w2, rw_a0, rw_a2, rw_g2, rw_k_k, rw_k_a, rw_r_k,
              rw_gn_g, rw_gn_b, w_in_odd, m_conv_w, m_conv_b, m_dt_bias, m_A_log,
              m_D, m_norm_g, w_out_odd):
    P = dict(ffn1_in=ffn1_in, ffn1_out=ffn1_out, ffn2_in=ffn2_in, ffn2_out=ffn2_out,
             ln_g=ln_g, ln_b=ln_b, w_in_even=w_in_even, w_out_even=w_out_even,
             mla_q_norm=mla_q_norm, mla_w_uq=mla_w_uq, mla_kv_norm=mla_kv_norm,
             mla_w_ukv=mla_w_ukv, rw_mu=rw_mu, rw_w0=rw_w0, rw_w2=rw_w2, rw_a0=rw_a0,
             rw_a2=rw_a2, rw_g2=rw_g2, rw_k_k=rw_k_k, rw_k_a=rw_k_a, rw_r_k=rw_r_k,
             rw_gn_g=rw_gn_g, rw_gn_b=rw_gn_b, w_in_odd=w_in_odd, m_conv_w=m_conv_w,
             m_conv_b=m_conv_b, m_dt_bias=m_dt_bias, m_A_log=m_A_log, m_D=m_D,
             m_norm_g=m_norm_g, w_out_odd=w_out_odd)
    y_prompt = trunk(x_prompt, P)
    y_sample = trunk(x_sample, P)
    return (y_prompt, y_sample)
```

```python
import functools
import math

import jax
import jax.numpy as jnp
from jax import lax
from jax.experimental import pallas as pl
from jax.experimental.pallas import tpu as pltpu

F32 = jnp.float32
BF16 = jnp.bfloat16

D_MODEL = 1024
DEPTH = 4
DN_ALPHA = (2.0 * DEPTH) ** 0.25
LN_EPS = 1e-5
RMS_EPS = 1e-6
D_FF = 2816

MLA_HEADS = 8
MLA_NOPE = 64
MLA_ROPE = 32
MLA_QK = MLA_NOPE + MLA_ROPE
MLA_V = 64
Q_LORA = 384
KV_LORA = 256
ROPE_THETA = 10000.0
MLA_SCALE = MLA_QK ** -0.5
MLA_IN = Q_LORA + KV_LORA + MLA_ROPE

RW_HEADS = 8
RW_HEAD = 64
RW_WIDTH = RW_HEADS * RW_HEAD
DECAY_LORA = 64
AAA_LORA = 64
GATE_LORA = 128
RW_GN_EPS = 64e-5
RW_LORA = 2 * DECAY_LORA + AAA_LORA + GATE_LORA
RW_LORA_PAD = 384

M_INNER = 2 * D_MODEL
M_HEADDIM = 64
M_HEADS = M_INNER // M_HEADDIM
M_GROUPS = 4
M_HPG = M_HEADS // M_GROUPS
M_STATE = 128
M_CONV = 5
SSD_CHUNK = 128
M_CONV_DIM = M_INNER + 2 * M_GROUPS * M_STATE
M_GROUP_W = M_INNER // M_GROUPS

LANES = 128
SUBLANES = 8
HALO = SUBLANES
VMEM_LIMIT = 56 * 1024 * 1024

RW_CHUNK = 64
RW_INV_BLOCK = 16


def _tile(n, pref):
    t = min(n, pref)
    while n % t:
        t -= SUBLANES
    return t


def _cparams(sem):
    return pltpu.CompilerParams(dimension_semantics=sem, vmem_limit_bytes=VMEM_LIMIT)


def _ln(y, g, b):
    mu = jnp.mean(y, axis=-1, keepdims=True)
    d = y - mu
    var = jnp.mean(d * d, axis=-1, keepdims=True)
    return d * lax.rsqrt(var + LN_EPS) * g + b


def _bdot(a, b):
    return jnp.dot(a.astype(BF16), b.astype(BF16), preferred_element_type=F32)


def _split3(a):
    hi = a.astype(BF16)
    r1 = a - hi.astype(F32)
    mid = r1.astype(BF16)
    lo = (r1 - mid.astype(F32)).astype(BF16)
    return hi, mid, lo


def _split2(a):
    hi = a.astype(BF16)
    lo = (a - hi.astype(F32)).astype(BF16)
    return hi, lo


_NN = (((1,), (0,)), ((), ()))
_NT = (((1,), (1,)), ((), ()))


def _dg(a, b, dims):
    return lax.dot_general(a, b, dims, preferred_element_type=F32)


def _dot3(a, b, dims=_NN):
    ah, al = _split2(a)
    bh, bl = _split2(b)
    return _dg(ah, bh, dims) + (_dg(ah, bl, dims) + _dg(al, bh, dims))


def _dot_exact_lhs(a_bf16, b):
    b0, b1, b2 = _split3(b)
    return _dg(a_bf16, b0, _NN) + (_dg(a_bf16, b1, _NN) + _dg(a_bf16, b2, _NN))


def _dot_exact_rhs(a, b_bf16):
    a0, a1, a2 = _split3(a)
    return _dg(a0, b_bf16, _NN) + (_dg(a1, b_bf16, _NN) + _dg(a2, b_bf16, _NN))


def _ffn_kernel(x_ref, wg_ref, wu_ref, wo_ref, g_ref, b_ref, o_ref, acc_ref, *, n_ff):
    j = pl.program_id(1)
    xb = x_ref[...].astype(BF16)
    gate = jnp.dot(xb, wg_ref[...], preferred_element_type=F32)
    up = jnp.dot(xb, wu_ref[...], preferred_element_type=F32)
    h = (gate * jax.nn.sigmoid(gate) * up).astype(BF16)
    part = jnp.dot(h, wo_ref[...], preferred_element_type=F32)

    @pl.when(j == 0)
    def _():
        acc_ref[...] = part

    @pl.when(j > 0)
    def _():
        acc_ref[...] += part

    @pl.when(j == n_ff - 1)
    def _():
        y = DN_ALPHA * x_ref[...] + 0.5 * acc_ref[...]
        o_ref[...] = _ln(y, g_ref[...], b_ref[...])


def _ffn_ln(x, w_in, w_out, g, b, *, tm_pref=512, fc=256):
    T = x.shape[0]
    tm = _tile(T, tm_pref)
    n_ff = D_FF // fc
    w_in = w_in.astype(BF16)
    w_out = w_out.astype(BF16)
    return pl.pallas_call(
        functools.partial(_ffn_kernel, n_ff=n_ff),
        grid=(T // tm, n_ff),
        in_specs=[
            pl.BlockSpec((tm, D_MODEL), lambda i, j: (i, 0)),
            pl.BlockSpec((D_MODEL, fc), lambda i, j: (0, j)),
            pl.BlockSpec((D_MODEL, fc), lambda i, j: (0, j + n_ff)),
            pl.BlockSpec((fc, D_MODEL), lambda i, j: (j, 0)),
            pl.BlockSpec((1, D_MODEL), lambda i, j: (0, 0)),
            pl.BlockSpec((1, D_MODEL), lambda i, j: (0, 0)),
        ],
        out_specs=pl.BlockSpec((tm, D_MODEL), lambda i, j: (i, 0)),
        out_shape=jax.ShapeDtypeStruct((T, D_MODEL), F32),
        scratch_shapes=[pltpu.VMEM((tm, D_MODEL), F32)],
        compiler_params=_cparams(("parallel", "arbitrary")),
        name="ffn_ln",
    )(x, w_in, w_in, w_out, g.reshape(1, -1), b.reshape(1, -1))


def _even_proj_kernel(x_ref, wq_ref, wkv_ref, wkra_ref, wkrb_ref, wrkv_ref, wlora_ref,
                      qg_ref, kvg_ref, wuq_ref, wuqs_ref, wuk_ref, wuv_ref,
                      cos_ref, sin_ref,
                      q_ref, k_ref, v_ref, rkv_ref, lora_ref):
    xb = x_ref[...].astype(BF16)
    rkv_ref[...] = jnp.dot(xb, wrkv_ref[...], preferred_element_type=F32)
    lora_ref[...] = jnp.dot(xb, wlora_ref[...], preferred_element_type=F32)

    cq = jnp.dot(xb, wq_ref[...], preferred_element_type=F32)
    ckv = jnp.dot(xb, wkv_ref[...], preferred_element_type=F32)
    cqn = (cq * lax.rsqrt(jnp.mean(cq * cq, axis=-1, keepdims=True) + RMS_EPS) * qg_ref[...]).astype(BF16)
    ckvn = (ckv * lax.rsqrt(jnp.mean(ckv * ckv, axis=-1, keepdims=True) + RMS_EPS) * kvg_ref[...]).astype(BF16)

    cos = cos_ref[...]
    sin = sin_ref[...]
    lane = lax.broadcasted_iota(jnp.int32, cos.shape, 1)
    cos_q = jnp.where(lane < MLA_NOPE, 1.0, cos)

    kr = (jnp.dot(xb, wkra_ref[...], preferred_element_type=F32) * cos
          + jnp.dot(xb, wkrb_ref[...], preferred_element_type=F32) * sin)

    qa = jnp.dot(cqn, wuq_ref[...], preferred_element_type=F32)
    qs = jnp.dot(cqn, wuqs_ref[...], preferred_element_type=F32)
    kn = jnp.dot(ckvn, wuk_ref[...], preferred_element_type=F32)
    vv = jnp.dot(ckvn, wuv_ref[...], preferred_element_type=F32)
    for h in range(MLA_HEADS):
        sl = slice(h * LANES, (h + 1) * LANES)
        q_ref[h] = ((qa[:, sl] * cos_q + qs[:, sl] * sin) * MLA_SCALE).astype(BF16)
        k_ref[h] = (kn[:, sl] + kr).astype(BF16)
    for p in range(MLA_HEADS // 2):
        v_ref[p] = vv[:, p * LANES:(p + 1) * LANES].astype(BF16)


def _rope_swap(w):
    half = MLA_ROPE // 2
    return jnp.concatenate([-w[..., half:], w[..., :half]], axis=-1)


def _even_proj(x, S, w_in, q_norm, w_uq, kv_norm, w_ukv, *, tm_pref=256):
    T = x.shape[0]
    tm = _tile(S, tm_pref)
    nt_seq = S // tm
    H = MLA_HEADS

    wq = w_in[:, :Q_LORA]
    wkv = w_in[:, Q_LORA:Q_LORA + KV_LORA]
    wkr = w_in[:, Q_LORA + KV_LORA:MLA_IN]
    zk = jnp.zeros((D_MODEL, MLA_NOPE), F32)
    zt = jnp.zeros((D_MODEL, LANES - MLA_QK), F32)
    wkra = jnp.concatenate([zk, wkr, zt], axis=1)
    wkrb = jnp.concatenate([zk, _rope_swap(wkr), zt], axis=1)
    wrw = w_in[:, MLA_IN:]
    wrkv = wrw[:, :3 * RW_WIDTH]
    wlora = jnp.pad(wrw[:, 3 * RW_WIDTH:], ((0, 0), (0, RW_LORA_PAD - RW_LORA)))

    uq = w_uq.reshape(Q_LORA, H, MLA_QK)
    pad_q = jnp.zeros((Q_LORA, H, LANES - MLA_QK), F32)
    wuq = jnp.concatenate([uq, pad_q], axis=-1).reshape(Q_LORA, H * LANES)
    wuqs = jnp.concatenate([jnp.zeros((Q_LORA, H, MLA_NOPE), F32), _rope_swap(uq[..., MLA_NOPE:]), pad_q],
                           axis=-1).reshape(Q_LORA, H * LANES)
    ukv = w_ukv.reshape(KV_LORA, H, MLA_NOPE + MLA_V)
    wuk = jnp.concatenate([ukv[..., :MLA_NOPE], jnp.zeros((KV_LORA, H, LANES - MLA_NOPE), F32)],
                          axis=-1).reshape(KV_LORA, H * LANES)
    wuv = ukv[..., MLA_NOPE:].reshape(KV_LORA, H * MLA_V)

    half = MLA_ROPE // 2
    inv = ROPE_THETA ** (-jnp.arange(half, dtype=F32) / half)
    ang = jnp.arange(S, dtype=F32)[:, None] * inv[None, :]
    zl = jnp.zeros((S, MLA_NOPE), F32)
    zr = jnp.zeros((S, LANES - MLA_QK), F32)
    cos_t = jnp.concatenate([zl, jnp.cos(ang), jnp.cos(ang), zr], axis=1)
    sin_t = jnp.concatenate([zl, jnp.sin(ang), jnp.sin(ang), zr], axis=1)

    def const(shape):
        return pl.BlockSpec(shape, lambda i: (0,) * len(shape))

    bw = lambda a: a.astype(BF16)
    outs = pl.pallas_call(
        _even_proj_kernel,
        grid=(T // tm,),
        in_specs=[
            pl.BlockSpec((tm, D_MODEL), lambda i: (i, 0)),
            const((D_MODEL, Q_LORA)), const((D_MODEL, KV_LORA)),
            const((D_MODEL, LANES)), const((D_MODEL, LANES)),
            const((D_MODEL, 3 * RW_WIDTH)), const((D_MODEL, RW_LORA_PAD)),
            const((1, Q_LORA)), const((1, KV_LORA)),
            const((Q_LORA, H * LANES)), const((Q_LORA, H * LANES)),
            const((KV_LORA, H * LANES)), const((KV_LORA, H * MLA_V)),
            pl.BlockSpec((tm, LANES), lambda i: (i % nt_seq, 0)),
            pl.BlockSpec((tm, LANES), lambda i: (i % nt_seq, 0)),
        ],
        out_specs=[
            pl.BlockSpec((H, tm, LANES), lambda i: (0, i, 0)),
            pl.BlockSpec((H, tm, LANES), lambda i: (0, i, 0)),
            pl.BlockSpec((H // 2, tm, LANES), lambda i: (0, i, 0)),
            pl.BlockSpec((tm, 3 * RW_WIDTH), lambda i: (i, 0)),
            pl.BlockSpec((tm, RW_LORA_PAD), lambda i: (i, 0)),
        ],
        out_shape=[
            jax.ShapeDtypeStruct((H, T, LANES), BF16),
            jax.ShapeDtypeStruct((H, T, LANES), BF16),
            jax.ShapeDtypeStruct((H // 2, T, LANES), BF16),
            jax.ShapeDtypeStruct((T, 3 * RW_WIDTH), F32),
            jax.ShapeDtypeStruct((T, RW_LORA_PAD), F32),
        ],
        compiler_params=_cparams(("parallel",)),
        name="even_proj",
    )(x, bw(wq), bw(wkv), bw(wkra), bw(wkrb), bw(wrkv), bw(wlora),
      q_norm.reshape(1, -1), kv_norm.reshape(1, -1),
      bw(wuq), bw(wuqs), bw(wuk), bw(wuv), cos_t, sin_t)
    return outs


def _attn_kernel(q_ref, k_ref, v_ref, o_ref, *, n_kv, tk):
    qa = q_ref[0]
    qb = q_ref[1]
    tq = qa.shape[0]

    def one_head(q, kk, vv, m, l, acc):
        s = lax.dot_general(q, kk, _NT, preferred_element_type=F32)
        m_new = jnp.maximum(m, jnp.max(s, axis=-1, keepdims=True))
        alpha = jnp.exp(m - m_new)
        p = jnp.exp(s - m_new)
        l = alpha * l + jnp.sum(p, axis=-1, keepdims=True)
        acc = alpha * acc + jnp.dot(p.astype(BF16), vv, preferred_element_type=F32)
        return m_new, l, acc

    def body(c, carry):
        ma, la, acca, mb, lb, accb = carry
        rows = pl.ds(pl.multiple_of(c * tk, tk), tk)
        vv = v_ref[0, rows, :]
        ma, la, acca = one_head(qa, k_ref[0, rows, :], vv, ma, la, acca)
        mb, lb, accb = one_head(qb, k_ref[1, rows, :], vv, mb, lb, accb)
        return ma, la, acca, mb, lb, accb

    m0 = jnp.full((tq, 1), -jnp.inf, F32)
    l0 = jnp.zeros((tq, 1), F32)
    a0 = jnp.zeros((tq, LANES), F32)
    ma, la, acca, mb, lb, accb = lax.fori_loop(0, n_kv, body, (m0, l0, a0, m0, l0, a0))
    lane = lax.broadcasted_iota(jnp.int32, (tq, LANES), 1)
    o_ref[...] = jnp.where(lane < MLA_V, acca / la, accb / lb).astype(o_ref.dtype)


def _attention(q, k, v, B, S, *, tq_pref=256, tk_pref=512):
    H, T, _ = q.shape
    tq = _tile(S, tq_pref)
    tk = _tile(S, tk_pref)
    nq = S // tq
    return pl.pallas_call(
        functools.partial(_attn_kernel, n_kv=S // tk, tk=tk),
        grid=(B, H // 2, nq),
        in_specs=[
            pl.BlockSpec((2, tq, LANES), lambda b, p, i: (p, b * nq + i, 0)),
            pl.BlockSpec((2, S, LANES), lambda b, p, i: (p, b, 0)),
            pl.BlockSpec((1, S, LANES), lambda b, p, i: (p, b, 0)),
        ],
        out_specs=pl.BlockSpec((tq, LANES), lambda b, p, i: (b * nq + i, p)),
        out_shape=jax.ShapeDtypeStruct((T, H * MLA_V), BF16),
        compiler_params=_cparams(("parallel", "parallel", "arbitrary")),
        name="mla_attention",
    )(q, k, v)


def _shifted(main_ref, prev_ref, next_ref, ext_ref, first, last, reach):
    tm = main_ref.shape[0]
    ext_ref[0:HALO, :] = jnp.where(first, 0.0, prev_ref[...])
    ext_ref[HALO:HALO + tm, :] = main_ref[...]
    ext_ref[HALO + tm:2 * HALO + tm, :] = jnp.where(last, 0.0, next_ref[...])
    del reach

    def load(d):
        return ext_ref[HALO + d:HALO + d + tm, :]

    return load


def _head_sum(x, ones_ref):
    return _dot_exact_rhs(x, ones_ref[...])


def _rwkv_prep_kernel(rkv_ref, rkvp_ref, rkvn_ref, lo_ref, lop_ref, lon_ref,
                      mu_rkv_ref, mu_lo_ref, wl_ref, bias_ref, vec_ref, ones_ref,
                      r_ref, k_ref, v_ref, kk_ref, kka_ref, lw_ref, bonus_ref, g_ref,
                      ext_rkv, ext_lo, *, nt_seq):
    i = pl.program_id(0)
    first = (i % nt_seq) == 0
    last = (i % nt_seq) == nt_seq - 1
    W = RW_WIDTH

    ld = _shifted(rkv_ref, rkvp_ref, rkvn_ref, ext_rkv, first, last, 1)
    cur = rkv_ref[...]
    rkv = cur + (0.5 * (ld(-1) + ld(1)) - cur) * mu_rkv_ref[...]
    ld2 = _shifted(lo_ref, lop_ref, lon_ref, ext_lo, first, last, 1)
    cur2 = lo_ref[...]
    lo = cur2 + (0.5 * (ld2(-1) + ld2(1)) - cur2) * mu_lo_ref[...]

    r = rkv[:, 0:W]
    k = rkv[:, W:2 * W]
    v = rkv[:, 2 * W:3 * W]

    lane = lax.broadcasted_iota(jnp.int32, lo.shape, 1)
    act = jnp.where(lane < 2 * DECAY_LORA, jnp.tanh(lo),
                    jnp.where(lane < 2 * DECAY_LORA + AAA_LORA, lo, jax.nn.sigmoid(lo)))
    proj = jnp.dot(act.astype(BF16), wl_ref[...], preferred_element_type=F32) + bias_ref[...]
    u_f = proj[:, 0:W]
    u_b = proj[:, W:2 * W]
    a = jax.nn.sigmoid(proj[:, 2 * W:3 * W])
    g = proj[:, 3 * W:4 * W]

    k_k = vec_ref[0:1, :]
    k_a = vec_ref[1:2, :]
    r_k = vec_ref[2:3, :]

    c = math.exp(-0.5)
    lw_ref[0] = -c * jax.nn.sigmoid(u_f)
    lw_ref[1] = -c * jax.nn.sigmoid(u_b)

    kx = k * k_k
    kk = kx * lax.rsqrt(_head_sum(kx * kx, ones_ref) + 1e-12)
    k2 = k * (1.0 + (a - 1.0) * k_a)

    r_ref[...] = r
    k_ref[...] = k2
    v_ref[...] = v
    kk_ref[...] = kk
    kka_ref[...] = kk * a
    bonus_ref[...] = _head_sum(r * k2 * r_k, ones_ref) * v
    g_ref[...] = g


def _rwkv_prep(rkv, lora, S, mu, w0, w2, a0, a2, g2, k_k, k_a, r_k, ones_blk, *, tm_pref=256):
    T = rkv.shape[0]
    tm = _tile(S, tm_pref)
    nt_seq = S // tm
    W = RW_WIDTH
    hb = tm // HALO
    last_hb = T // HALO - 1

    mu_rkv = mu[:3 * W].reshape(1, -1)
    mu_lo = jnp.pad(mu[3 * W:], (0, RW_LORA_PAD - RW_LORA)).reshape(1, -1)
    wl = jnp.zeros((RW_LORA_PAD, 4 * W), F32)
    o1, o2, o3 = DECAY_LORA, 2 * DECAY_LORA, 2 * DECAY_LORA + AAA_LORA
    wl = wl.at[0:o1, 0:W].set(w2[0])
    wl = wl.at[o1:o2, W:2 * W].set(w2[1])
    wl = wl.at[o2:o3, 2 * W:3 * W].set(a2)
    wl = wl.at[o3:o3 + GATE_LORA, 3 * W:4 * W].set(g2)
    bias = jnp.concatenate([w0[0], w0[1], a0, jnp.zeros((W,), F32)]).reshape(1, -1)
    vecs = jnp.concatenate([k_k.reshape(1, W), k_a.reshape(1, W), r_k.reshape(1, W),
                            jnp.zeros((SUBLANES - 3, W), F32)], axis=0)

    def tiled(c):
        return pl.BlockSpec((tm, c), lambda i: (i, 0))

    def prev(c):
        return pl.BlockSpec((HALO, c), lambda i: (jnp.maximum(i * hb - 1, 0), 0))

    def nxt(c):
        return pl.BlockSpec((HALO, c), lambda i: (jnp.minimum((i + 1) * hb, last_hb), 0))

    def const(shape):
        return pl.BlockSpec(shape, lambda i: (0,) * len(shape))

    o512 = jax.ShapeDtypeStruct((T, W), F32)
    return pl.pallas_call(
        functools.partial(_rwkv_prep_kernel, nt_seq=nt_seq),
        grid=(T // tm,),
        in_specs=[tiled(3 * W), prev(3 * W), nxt(3 * W),
                  tiled(RW_LORA_PAD), prev(RW_LORA_PAD), nxt(RW_LORA_PAD),
                  const((1, 3 * W)), const((1, RW_LORA_PAD)),
                  const((RW_LORA_PAD, 4 * W)), const((1, 4 * W)),
                  const((SUBLANES, W)), const((W, W))],
        out_specs=[tiled(W), tiled(W), tiled(W), tiled(W), tiled(W),
                   pl.BlockSpec((2, tm, W), lambda i: (0, i, 0)),
                   tiled(W), tiled(W)],
        out_shape=[o512, o512, o512, o512, o512,
                   jax.ShapeDtypeStruct((2, T, W), F32), o512, o512],
        scratch_shapes=[pltpu.VMEM((tm + 2 * HALO, 3 * W), F32),
                        pltpu.VMEM((tm + 2 * HALO, RW_LORA_PAD), F32)],
        compiler_params=_cparams(("parallel",)),
        name="rwkv_prep",
    )(rkv, rkv, rkv, lora, lora, lora, mu_rkv, mu_lo, wl.astype(BF16), bias, vecs, ones_blk)


def _unit_tri_inverse(M, same_blk, tri_strict):
    n = M.shape[0]
    row = lax.broadcasted_iota(jnp.int32, (n, n), 0)
    col = lax.broadcasted_iota(jnp.int32, (n, n), 1)
    eye = jnp.where(row == col, 1.0, 0.0).astype(F32)
    del tri_strict
    Md = jnp.where(same_blk(RW_INV_BLOCK), M, 0.0)
    M2 = _dot3(Md, Md)
    M4 = _dot3(M2, M2)
    M8 = _dot3(M4, M4)
    T = _dot3(eye - Md, eye + M2)
    T = _dot3(T, eye + M4)
    T = _dot3(T, eye + M8)
    s = RW_INV_BLOCK
    while s < n:
        C = jnp.where(jnp.logical_and(same_blk(2 * s), jnp.logical_not(same_blk(s))), M, 0.0)
        T = T - _dot3(_dot3(T, C), T)
        s *= 2
    return T


def _rwkv_scan_kernel(r_ref, k_ref, v_ref, kk_ref, kka_ref, lw_ref, y_ref, s_ref, *, n_sub):
    d = pl.program_id(0)
    c = pl.program_id(2)
    L = RW_CHUNK
    N = RW_HEAD
    fwd = d == 0

    @pl.when(c == 0)
    def _():
        s_ref[...] = jnp.zeros_like(s_ref)

    row = lax.broadcasted_iota(jnp.int32, (L, L), 0)
    col = lax.broadcasted_iota(jnp.int32, (L, L), 1)
    ahead = (row - col) * (1 - 2 * d)
    earlier = ahead > 0
    upto = ahead >= 0
    cum_mask = jnp.where(upto, 1.0, 0.0).astype(BF16)

    def same_blk(s):
        return (row // s) == (col // s)

    def sub_chunk(sidx, carry):
        j = jnp.where(fwd, sidx, n_sub - 1 - sidx)
        rows = pl.ds(pl.multiple_of(j * L, L), L)
        lw = lw_ref[0, rows, :]
        cs = _dot_exact_lhs(cum_mask, lw)
        e_neg = jnp.exp(-cs)
        Ah = kka_ref[rows, :] * e_neg
        Kh = k_ref[rows, :] * e_neg
        kap = kk_ref[rows, :] * jnp.exp(cs - lw)
        Rh = r_ref[rows, :] * jnp.exp(cs)
        V = v_ref[rows, :]
        w_tot = jnp.exp(jnp.where(fwd, cs[L - 1:L, :], cs[0:1, :]))

        ys = []
        for h in range(RW_HEADS):
            sl = slice(h * N, (h + 1) * N)
            X = jnp.concatenate([kap[:, sl], Rh[:, sl]], axis=0)
            Z = jnp.concatenate([Ah[:, sl], Kh[:, sl]], axis=0)
            Vh = V[:, sl]
            S0 = s_ref[h]
            G = _dot3(X, Z, _NT)
            Mak = jnp.where(earlier, G[0:L, 0:L], 0.0)
            Mkk = jnp.where(earlier, G[0:L, L:2 * L], 0.0)
            Mar = jnp.where(upto, G[L:2 * L, 0:L], 0.0)
            Mkr = jnp.where(upto, G[L:2 * L, L:2 * L], 0.0)
            P0 = _dot3(X, S0, _NT)
            Bm = -(P0[0:L] + _dot3(Mkk, Vh))
            Tinv = _unit_tri_inverse(Mak, same_blk, earlier)
            U = _dot3(Tinv, Bm)
            UV = jnp.concatenate([U, Vh], axis=0)
            Y = P0[L:2 * L] + _dot3(jnp.concatenate([Mar, Mkr], axis=1), UV)
            s_ref[h] = (S0 + _dot3(UV.T, Z)) * w_tot[:, sl]
            ys.append(Y)
        y_ref[0, rows, :] = jnp.concatenate(ys, axis=1)
        return carry

    lax.fori_loop(0, n_sub, sub_chunk, 0)


def _rwkv_scan(r, k, v, kk, kka, lw, B, S, *, tb_pref=256):
    T, W = r.shape
    tb = _tile(S, tb_pref)
    nb = S // tb

    def tmap(d, b, c):
        return (b * nb + jnp.where(d == 0, c, nb - 1 - c), 0)

    tiled = pl.BlockSpec((tb, W), tmap)
    dir_tiled = pl.BlockSpec((1, tb, W), lambda d, b, c: (d,) + tmap(d, b, c))
    return pl.pallas_call(
        functools.partial(_rwkv_scan_kernel, n_sub=tb // RW_CHUNK),
        grid=(2, B, nb),
        in_specs=[tiled, tiled, tiled, tiled, tiled, dir_tiled],
        out_specs=dir_tiled,
        out_shape=jax.ShapeDtypeStruct((2, T, W), F32),
        scratch_shapes=[pltpu.VMEM((RW_HEADS, RW_HEAD, RW_HEAD), F32)],
        compiler_params=_cparams(("parallel", "parallel", "arbitrary")),
        name="rwkv_scan",
    )(r, k, v, kk, kka, lw)


def _even_out_kernel(x_ref, oa_ref, y_ref, bonus_ref, g_ref, gn_ref, ones_ref,
                     wa_ref, wb_ref, lg_ref, lb_ref, o_ref):
    y = y_ref[0] + y_ref[1]
    inv_n = 1.0 / RW_HEAD
    mean = _head_sum(y, ones_ref) * inv_n
    dy = y - mean
    var = _head_sum(dy * dy, ones_ref) * inv_n
    yn = dy * lax.rsqrt(var + RW_GN_EPS) * gn_ref[0:1, :] + gn_ref[1:2, :]
    ob = ((yn + bonus_ref[...]) * g_ref[...]).astype(BF16)
    m = (jnp.dot(oa_ref[...], wa_ref[...], preferred_element_type=F32)
         + jnp.dot(ob, wb_ref[...], preferred_element_type=F32))
    o_ref[...] = _ln(DN_ALPHA * x_ref[...] + m, lg_ref[...], lb_ref[...])


def _even_out(x, oa, y, bonus, g, gn_g, gn_b, ones_blk, w_out, ln_g, ln_b, *, tm_pref=256):
    T = x.shape[0]
    tm = _tile(T, tm_pref)
    W = RW_WIDTH
    gn = jnp.concatenate([gn_g.reshape(1, W), gn_b.reshape(1, W), jnp.zeros((SUBLANES - 2, W), F32)], axis=0)
    wa = w_out[:MLA_HEADS * MLA_V].astype(BF16)
    wb = w_out[MLA_HEADS * MLA_V:].astype(BF16)

    def tiled(c):
        return pl.BlockSpec((tm, c), lambda i: (i, 0))

    def const(shape):
        return pl.BlockSpec(shape, lambda i: (0,) * len(shape))

    return pl.pallas_call(
        _even_out_kernel,
        grid=(T // tm,),
        in_specs=[tiled(D_MODEL), tiled(MLA_HEADS * MLA_V),
                  pl.BlockSpec((2, tm, W), lambda i: (0, i, 0)),
                  tiled(W), tiled(W), const((SUBLANES, W)), const((W, W)),
                  const((MLA_HEADS * MLA_V, D_MODEL)), const((W, D_MODEL)),
                  const((1, D_MODEL)), const((1, D_MODEL))],
        out_specs=tiled(D_MODEL),
        out_shape=jax.ShapeDtypeStruct((T, D_MODEL), F32),
        compiler_params=_cparams(("parallel",)),
        name="even_out",
    )(x, oa, y, bonus, g, gn, ones_blk, wa, wb, ln_g.reshape(1, -1), ln_b.reshape(1, -1))


def _matmul_kernel(x_ref, w_ref, o_ref):
    o_ref[...] = jnp.dot(x_ref[...].astype(BF16), w_ref[...], preferred_element_type=F32).astype(o_ref.dtype)


def _matmul(x, w, *, tm_pref=512, tn_pref=1024, out_dtype=F32, name="matmul"):
    T, K = x.shape
    N = w.shape[1]
    tm = _tile(T, tm_pref)
    tn = N if N <= tn_pref else tn_pref
    assert N % tn == 0
    return pl.pallas_call(
        _matmul_kernel,
        grid=(T // tm, N // tn),
        in_specs=[pl.BlockSpec((tm, K), lambda i, j: (i, 0)),
                  pl.BlockSpec((K, tn), lambda i, j: (0, j))],
        out_specs=pl.BlockSpec((tm, tn), lambda i, j: (i, j)),
        out_shape=jax.ShapeDtypeStruct((T, N), out_dtype),
        compiler_params=_cparams(("parallel", "arbitrary")),
        name=name,
    )(x, w.astype(BF16))


def _conv_kernel(u_ref, up_ref, un_ref, w_ref, b_ref, xs_ref, bm_ref, cm_ref, ext_ref, *, nt_seq):
    i = pl.program_id(0)
    first = (i % nt_seq) == 0
    last = (i % nt_seq) == nt_seq - 1
    ld = _shifted(u_ref, up_ref, un_ref, ext_ref, first, last, M_CONV // 2)
    acc = b_ref[...] + ld(-2) * w_ref[0:1, :]
    for j in range(1, M_CONV):
        acc = acc + ld(j - M_CONV // 2) * w_ref[j:j + 1, :]
    y = acc * jax.nn.sigmoid(acc)
    xs_ref[...] = y[:, 0:M_INNER]
    bm_ref[...] = y[:, M_INNER:M_INNER + M_GROUPS * M_STATE]
    cm_ref[...] = y[:, M_INNER + M_GROUPS * M_STATE:]


def _conv_silu(xbc, S, conv_w, conv_b, *, tm_pref=256):
    T, C = xbc.shape
    tm = _tile(S, tm_pref)
    nt_seq = S // tm
    hb = tm // HALO
    last_hb = T // HALO - 1
    wpad = jnp.concatenate([conv_w, jnp.zeros((SUBLANES - M_CONV, C), F32)], axis=0)
    GN = M_GROUPS * M_STATE
    return pl.pallas_call(
        functools.partial(_conv_kernel, nt_seq=nt_seq),
        grid=(T // tm,),
        in_specs=[pl.BlockSpec((tm, C), lambda i: (i, 0)),
                  pl.BlockSpec((HALO, C), lambda i: (jnp.maximum(i * hb - 1, 0), 0)),
                  pl.BlockSpec((HALO, C), lambda i: (jnp.minimum((i + 1) * hb, last_hb), 0)),
                  pl.BlockSpec((SUBLANES, C), lambda i: (0, 0)),
                  pl.BlockSpec((1, C), lambda i: (0, 0))],
        out_specs=[pl.BlockSpec((tm, M_INNER), lambda i: (i, 0)),
                   pl.BlockSpec((tm, GN), lambda i: (i, 0)),
                   pl.BlockSpec((tm, GN), lambda i: (i, 0))],
        out_shape=[jax.ShapeDtypeStruct((T, M_INNER), F32),
                   jax.ShapeDtypeStruct((T, GN), F32),
                   jax.ShapeDtypeStruct((T, GN), F32)],
        scratch_shapes=[pltpu.VMEM((tm + 2 * HALO, C), F32)],
        compiler_params=_cparams(("parallel",)),
        name="conv_silu",
    )(xbc, xbc, xbc, wpad, conv_b.reshape(1, -1))


def _ssd_kernel(xs_ref, bm_ref, cm_ref, dt_ref, dtb_ref, alog_ref, exp_ref, y_ref, st_ref):
    d = pl.program_id(0)
    c = pl.program_id(2)
    L = SSD_CHUNK
    P = M_HEADDIM
    GW = M_GROUP_W
    fwd = d == 0

    @pl.when(c == 0)
    def _():
        st_ref[...] = jnp.zeros_like(st_ref)

    row = lax.broadcasted_iota(jnp.int32, (L, L), 0)
    col = lax.broadcasted_iota(jnp.int32, (L, L), 1)
    upto = (row - col) * (1 - 2 * d) >= 0
    cum_mask = jnp.where(upto, 1.0, 0.0).astype(BF16)

    z = dt_ref[...] + dtb_ref[...]
    dt = jnp.maximum(z, 0.0) + jnp.log1p(jnp.exp(-jnp.abs(z)))
    a = dt * (-jnp.exp(alog_ref[...]))
    cs = _dot_exact_lhs(cum_mask, a)
    csT = cs.T
    tot = jnp.where(fwd, cs[L - 1:L, :], cs[0:1, :])

    expand = exp_ref[0]
    dt_x = _dot_exact_rhs(dt, expand)
    cs_x = _dot_exact_rhs(cs, expand)
    tot_x = _dot_exact_rhs(jnp.broadcast_to(tot, (SUBLANES, LANES)), expand)[0:1, :]

    xd = xs_ref[...] * dt_x
    xd_b = xd.astype(BF16)
    into_state = (xd * jnp.exp(tot_x - cs_x)).astype(BF16)
    from_state = jnp.exp(cs_x)

    ys = []
    for g in range(M_GROUPS):
        gs = slice(g * GW, (g + 1) * GW)
        ns = slice(g * M_STATE, (g + 1) * M_STATE)
        Bg = bm_ref[:, ns]
        Cg = cm_ref[:, ns].astype(BF16)
        CB = lax.dot_general(Cg, Bg.astype(BF16), _NT, preferred_element_type=F32)
        st = st_ref[g]
        y_off = jnp.dot(Cg, st.astype(BF16), preferred_element_type=F32) * from_state[:, gs]
        y_heads = []
        for hh in range(M_HPG):
            hidx = g * M_HPG + hh
            lane_f = hidx
            lane_b = M_HEADS + hidx
            col_l = jnp.where(fwd, cs[:, lane_f:lane_f + 1], cs[:, lane_b:lane_b + 1])
            row_s = jnp.where(fwd, csT[lane_f:lane_f + 1, :], csT[lane_b:lane_b + 1, :])
            dec = jnp.exp(jnp.where(upto, col_l - row_s, -jnp.inf))
            Mh = (CB * dec).astype(BF16)
            hs = slice(g * GW + hh * P, g * GW + (hh + 1) * P)
            y_heads.append(jnp.dot(Mh, xd_b[:, hs], preferred_element_type=F32))
        ys.append(jnp.concatenate(y_heads, axis=1) + y_off)
        st_ref[g] = st * jnp.exp(tot_x[:, gs]) + jnp.dot(Bg.T.astype(BF16), into_state[:, gs],
                                                          preferred_element_type=F32)
    y_ref[0] = jnp.concatenate(ys, axis=1)


def _ssd(xs, bm, cm, dt_raw, dt_bias, a_log, B, S):
    T = xs.shape[0]
    L = SSD_CHUNK
    nc = S // L
    GN = M_GROUPS * M_STATE

    def pad_lanes(v):
        return jnp.pad(v.reshape(1, -1), ((0, 0), (0, LANES - v.size)))

    heads = jnp.arange(M_INNER) // M_HEADDIM
    expand = jnp.stack([
        (jnp.arange(LANES)[:, None] == heads[None, :]),
        (jnp.arange(LANES)[:, None] == (heads[None, :] + M_HEADS)),
    ]).astype(BF16)

    def tmap(d, b, c):
        return (b * nc + jnp.where(d == 0, c, nc - 1 - c), 0)

    def tiled(w):
        return pl.BlockSpec((L, w), tmap)

    return pl.pallas_call(
        _ssd_kernel,
        grid=(2, B, nc),
        in_specs=[tiled(M_INNER), tiled(GN), tiled(GN), tiled(LANES),
                  pl.BlockSpec((1, LANES), lambda d, b, c: (0, 0)),
                  pl.BlockSpec((1, LANES), lambda d, b, c: (0, 0)),
                  pl.BlockSpec((1, LANES, M_INNER), lambda d, b, c: (d, 0, 0))],
        out_specs=pl.BlockSpec((1, L, M_INNER), lambda d, b, c: (d,) + tmap(d, b, c)),
        out_shape=jax.ShapeDtypeStruct((2, T, M_INNER), F32),
        scratch_shapes=[pltpu.VMEM((M_GROUPS, M_STATE, M_GROUP_W), F32)],
        compiler_params=_cparams(("parallel", "parallel", "arbitrary")),
        name="ssd_scan",
    )(xs, bm, cm, dt_raw, pad_lanes(dt_bias), pad_lanes(a_log), expand)


def _odd_out_kernel(x_ref, y_ref, xs_ref, z_ref, dvec_ref, ng_ref, w_ref, lg_ref, lb_ref, o_ref):
    z = z_ref[...]
    y = (y_ref[0] + y_ref[1] + xs_ref[...] * dvec_ref[...]) * (z * jax.nn.sigmoid(z))
    parts = []
    for g in range(M_GROUPS):
        yg = y[:, g * M_GROUP_W:(g + 1) * M_GROUP_W]
        parts.append(yg * lax.rsqrt(jnp.mean(yg * yg, axis=-1, keepdims=True) + RMS_EPS))
    yn = (jnp.concatenate(parts, axis=1) * ng_ref[...]).astype(BF16)
    m = jnp.dot(yn, w_ref[...], preferred_element_type=F32)
    o_ref[...] = _ln(DN_ALPHA * x_ref[...] + m, lg_ref[...], lb_ref[...])


def _odd_out(x, y, xs, z, d_skip, norm_g, w_out, ln_g, ln_b, *, tm_pref=256):
    T = x.shape[0]
    tm = _tile(T, tm_pref)
    dvec = jnp.repeat(d_skip, M_HEADDIM).reshape(1, -1)

    def tiled(c):
        return pl.BlockSpec((tm, c), lambda i: (i, 0))

    def const(shape):
        return pl.BlockSpec(shape, lambda i: (0,) * len(shape))

    return pl.pallas_call(
        _odd_out_kernel,
        grid=(T // tm,),
        in_specs=[tiled(D_MODEL), pl.BlockSpec((2, tm, M_INNER), lambda i: (0, i, 0)),
                  tiled(M_INNER), tiled(M_INNER), const((1, M_INNER)), const((1, M_INNER)),
                  const((M_INNER, D_MODEL)), const((1, D_MODEL)), const((1, D_MODEL))],
        out_specs=tiled(D_MODEL),
        out_shape=jax.ShapeDtypeStruct((T, D_MODEL), F32),
        compiler_params=_cparams(("parallel",)),
        name="odd_out",
    )(x, y, xs, z, dvec, norm_g.reshape(1, -1), w_out.astype(BF16),
      ln_g.reshape(1, -1), ln_b.reshape(1, -1))


def _even_mixer_ln(x, B, S, j, P, ln_g, ln_b, ones_blk):
    q, k, v, rkv, lora = _even_proj(x, S, P['w_in_even'][j], P['mla_q_norm'][j], P['mla_w_uq'][j],
                                    P['mla_kv_norm'][j], P['mla_w_ukv'][j])
    oa = _attention(q, k, v, B, S)
    r, k2, vv, kk, kka, lw, bonus, g = _rwkv_prep(
        rkv, lora, S, P['rw_mu'][j], P['rw_w0'][j], P['rw_w2'][j], P['rw_a0'][j], P['rw_a2'][j],
        P['rw_g2'][j], P['rw_k_k'][j], P['rw_k_a'][j], P['rw_r_k'][j], ones_blk)
    y = _rwkv_scan(r, k2, vv, kk, kka, lw, B, S)
    return _even_out(x, oa, y, bonus, g, P['rw_gn_g'][j], P['rw_gn_b'][j], ones_blk,
                     P['w_out_even'][j], ln_g, ln_b)


def _odd_mixer_ln(x, B, S, j, P, ln_g, ln_b):
    w_in = P['w_in_odd'][j]
    z = _matmul(x, w_in[:, :M_INNER], name="odd_proj_z")
    xbc = _matmul(x, w_in[:, M_INNER:M_INNER + M_CONV_DIM], name="odd_proj_xbc")
    w_dt = jnp.pad(w_in[:, M_INNER + M_CONV_DIM:], ((0, 0), (0, LANES - 2 * M_HEADS)))
    dt_raw = _matmul(x, w_dt, name="odd_proj_dt")
    xs, bm, cm = _conv_silu(xbc, S, P['m_conv_w'][j], P['m_conv_b'][j])
    y = _ssd(xs, bm, cm, dt_raw, P['m_dt_bias'][j].reshape(-1), P['m_A_log'][j].reshape(-1), B, S)
    return _odd_out(x, y, xs, z, P['m_D'][j], P['m_norm_g'][j], P['w_out_odd'][j], ln_g, ln_b)


def _trunk(x3, P):
    B, S, D = x3.shape
    x = x3.reshape(B * S, D)
    head_id = jnp.arange(RW_WIDTH) // RW_HEAD
    ones_blk = (head_id[:, None] == head_id[None, :]).astype(BF16)
    for i in range(DEPTH):
        x = _ffn_ln(x, P['ffn1_in'][i], P['ffn1_out'][i], P['ln_g'][i, 0], P['ln_b'][i, 0])
        if i % 2 == 0:
            x = _even_mixer_ln(x, B, S, i // 2, P, P['ln_g'][i, 1], P['ln_b'][i, 1], ones_blk)
        else:
            x = _odd_mixer_ln(x, B, S, i // 2, P, P['ln_g'][i, 1], P['ln_b'][i, 1])
        x = _ffn_ln(x, P['ffn2_in'][i], P['ffn2_out'][i], P['ln_g'][i, 2], P['ln_b'][i, 2])
    return x.reshape(B, S, D)


def kernel(x_prompt, x_sample, ffn1_in, ffn1_out, ffn2_in, ffn2_out, ln_g, ln_b, w_in_even, w_out_even, mla_q_norm, mla_w_uq, mla_kv_norm, mla_w_ukv, rw_mu, rw_w0, rw_w2, rw_a0, rw_a2, rw_g2, rw_k_k, rw_k_a, rw_r_k, rw_gn_g, rw_gn_b, w_in_odd, m_conv_w, m_conv_b, m_dt_bias, m_A_log, m_D, m_norm_g, w_out_odd):
    P = dict(ffn1_in=ffn1_in, ffn1_out=ffn1_out, ffn2_in=ffn2_in, ffn2_out=ffn2_out,
             ln_g=ln_g, ln_b=ln_b, w_in_even=w_in_even, w_out_even=w_out_even,
             mla_q_norm=mla_q_norm, mla_w_uq=mla_w_uq, mla_kv_norm=mla_kv_norm,
             mla_w_ukv=mla_w_ukv, rw_mu=rw_mu, rw_w0=rw_w0, rw_w2=rw_w2, rw_a0=rw_a0,
             rw_a2=rw_a2, rw_g2=rw_g2, rw_k_k=rw_k_k, rw_k_a=rw_k_a, rw_r_k=rw_r_k,
             rw_gn_g=rw_gn_g, rw_gn_b=rw_gn_b, w_in_odd=w_in_odd, m_conv_w=m_conv_w,
             m_conv_b=m_conv_b, m_dt_bias=m_dt_bias, m_A_log=m_A_log, m_D=m_D,
             m_norm_g=m_norm_g, w_out_odd=w_out_odd)
    return (_trunk(x_prompt, P), _trunk(x_sample, P))
```

```python
import functools
import math

import jax
import jax.numpy as jnp
from jax import lax
from jax.experimental import pallas as pl
from jax.experimental.pallas import tpu as pltpu

F32 = jnp.float32
BF16 = jnp.bfloat16

D_MODEL = 1024
DEPTH = 4
DN_ALPHA = (2.0 * DEPTH) ** 0.25
LN_EPS = 1e-5
RMS_EPS = 1e-6
D_FF = 2816

MLA_HEADS = 8
MLA_NOPE = 64
MLA_ROPE = 32
MLA_QK = MLA_NOPE + MLA_ROPE
MLA_V = 64
Q_LORA = 384
KV_LORA = 256
ROPE_THETA = 10000.0
MLA_SCALE = MLA_QK ** -0.5
MLA_IN = Q_LORA + KV_LORA + MLA_ROPE

RW_HEADS = 8
RW_HEAD = 64
RW_WIDTH = RW_HEADS * RW_HEAD
DECAY_LORA = 64
AAA_LORA = 64
GATE_LORA = 128
RW_GN_EPS = 64e-5
RW_LORA = 2 * DECAY_LORA + AAA_LORA + GATE_LORA
RW_LORA_PAD = 384

M_INNER = 2 * D_MODEL
M_HEADDIM = 64
M_HEADS = M_INNER // M_HEADDIM
M_GROUPS = 4
M_HPG = M_HEADS // M_GROUPS
M_STATE = 128
M_CONV = 5
SSD_CHUNK = 128
M_CONV_DIM = M_INNER + 2 * M_GROUPS * M_STATE
M_GROUP_W = M_INNER // M_GROUPS

LANES = 128
SUBLANES = 8
HALO = SUBLANES
VMEM_LIMIT = 56 * 1024 * 1024

RW_CHUNK = 64
RW_INV_BLOCK = 16


def _tile(n, pref):
    t = min(n, pref)
    while n % t:
        t -= SUBLANES
    return t


def _cparams(sem):
    return pltpu.CompilerParams(dimension_semantics=sem, vmem_limit_bytes=VMEM_LIMIT)


def _ln(y, g, b):
    mu = jnp.mean(y, axis=-1, keepdims=True)
    d = y - mu
    var = jnp.mean(d * d, axis=-1, keepdims=True)
    return d * lax.rsqrt(var + LN_EPS) * g + b


def _bdot(a, b):
    return jnp.dot(a.astype(BF16), b.astype(BF16), preferred_element_type=F32)


def _split3(a):
    hi = a.astype(BF16)
    r1 = a - hi.astype(F32)
    mid = r1.astype(BF16)
    lo = (r1 - mid.astype(F32)).astype(BF16)
    return hi, mid, lo


def _split2(a):
    hi = a.astype(BF16)
    lo = (a - hi.astype(F32)).astype(BF16)
    return hi, lo


_NN = (((1,), (0,)), ((), ()))
_NT = (((1,), (1,)), ((), ()))


def _dg(a, b, dims):
    return lax.dot_general(a, b, dims, preferred_element_type=F32)


def _dot3(a, b, dims=_NN):
    ah, al = _split2(a)
    bh, bl = _split2(b)
    return _dg(ah, bh, dims) + (_dg(ah, bl, dims) + _dg(al, bh, dims))


def _dot_exact_lhs(a_bf16, b):
    b0, b1, b2 = _split3(b)
    return _dg(a_bf16, b0, _NN) + (_dg(a_bf16, b1, _NN) + _dg(a_bf16, b2, _NN))


def _dot_exact_rhs(a, b_bf16):
    a0, a1, a2 = _split3(a)
    return _dg(a0, b_bf16, _NN) + (_dg(a1, b_bf16, _NN) + _dg(a2, b_bf16, _NN))


def _ffn_kernel(x_ref, wg_ref, wu_ref, wo_ref, g_ref, b_ref, o_ref, acc_ref, xb_ref, *, n_ff):
    j = pl.program_id(1)

    @pl.when(j == 0)
    def _():
        xb_ref[...] = x_ref[...].astype(BF16)

    xb = xb_ref[...]
    gate = jnp.dot(xb, wg_ref[...], preferred_element_type=F32)
    up = jnp.dot(xb, wu_ref[...], preferred_element_type=F32)
    h = (gate * jax.nn.sigmoid(gate) * up).astype(BF16)
    part = jnp.dot(h, wo_ref[...], preferred_element_type=F32)

    @pl.when(j == 0)
    def _():
        acc_ref[...] = part

    @pl.when(j > 0)
    def _():
        acc_ref[...] += part

    @pl.when(j == n_ff - 1)
    def _():
        y = DN_ALPHA * x_ref[...] + 0.5 * acc_ref[...]
        o_ref[...] = _ln(y, g_ref[...], b_ref[...])


def _ffn_ln(x, w_in, w_out, g, b, *, tm_pref=512, fc=1408):
    T = x.shape[0]
    tm = _tile(T, tm_pref)
    n_ff = D_FF // fc
    w_in = w_in.astype(BF16)
    w_out = w_out.astype(BF16)
    return pl.pallas_call(
        functools.partial(_ffn_kernel, n_ff=n_ff),
        grid=(T // tm, n_ff),
        in_specs=[
            pl.BlockSpec((tm, D_MODEL), lambda i, j: (i, 0)),
            pl.BlockSpec((D_MODEL, fc), lambda i, j: (0, j)),
            pl.BlockSpec((D_MODEL, fc), lambda i, j: (0, j + n_ff)),
            pl.BlockSpec((fc, D_MODEL), lambda i, j: (j, 0)),
            pl.BlockSpec((1, D_MODEL), lambda i, j: (0, 0)),
            pl.BlockSpec((1, D_MODEL), lambda i, j: (0, 0)),
        ],
        out_specs=pl.BlockSpec((tm, D_MODEL), lambda i, j: (i, 0)),
        out_shape=jax.ShapeDtypeStruct((T, D_MODEL), F32),
        scratch_shapes=[pltpu.VMEM((tm, D_MODEL), F32), pltpu.VMEM((tm, D_MODEL), BF16)],
        compiler_params=_cparams(("parallel", "arbitrary")),
        name="ffn_ln",
    )(x, w_in, w_in, w_out, g.reshape(1, -1), b.reshape(1, -1))


def _even_proj_kernel(x_ref, wq_ref, wkv_ref, wkra_ref, wkrb_ref, wrkv_ref, wlora_ref,
                      qg_ref, kvg_ref, wuq_ref, wuqs_ref, wuk_ref, wuv_ref,
                      cos_ref, sin_ref,
                      q_ref, k_ref, v_ref, rkv_ref, lora_ref):
    xb = x_ref[...].astype(BF16)
    rkv_ref[...] = jnp.dot(xb, wrkv_ref[...], preferred_element_type=F32)
    lora_ref[...] = jnp.dot(xb, wlora_ref[...], preferred_element_type=F32)

    cq = jnp.dot(xb, wq_ref[...], preferred_element_type=F32)
    ckv = jnp.dot(xb, wkv_ref[...], preferred_element_type=F32)
    cqn = (cq * lax.rsqrt(jnp.mean(cq * cq, axis=-1, keepdims=True) + RMS_EPS) * qg_ref[...]).astype(BF16)
    ckvn = (ckv * lax.rsqrt(jnp.mean(ckv * ckv, axis=-1, keepdims=True) + RMS_EPS) * kvg_ref[...]).astype(BF16)

    cos = cos_ref[...]
    sin = sin_ref[...]
    lane = lax.broadcasted_iota(jnp.int32, cos.shape, 1)
    cos_q = jnp.where(lane < MLA_NOPE, 1.0, cos)

    kr = (jnp.dot(xb, wkra_ref[...], preferred_element_type=F32) * cos
          + jnp.dot(xb, wkrb_ref[...], preferred_element_type=F32) * sin)

    qa = jnp.dot(cqn, wuq_ref[...], preferred_element_type=F32)
    qs = jnp.dot(cqn, wuqs_ref[...], preferred_element_type=F32)
    kn = jnp.dot(ckvn, wuk_ref[...], preferred_element_type=F32)
    vv = jnp.dot(ckvn, wuv_ref[...], preferred_element_type=F32)
    q_scale = MLA_SCALE * math.log2(math.e)
    one_col = jnp.where(lane == MLA_V, 1.0, 0.0)
    for h in range(MLA_HEADS):
        sl = slice(h * LANES, (h + 1) * LANES)
        q_ref[h] = ((qa[:, sl] * cos_q + qs[:, sl] * sin) * q_scale).astype(BF16)
        k_ref[h] = (kn[:, sl] + kr).astype(BF16)
        v_ref[h] = (vv[:, sl] + one_col).astype(BF16)


def _rope_swap(w):
    half = MLA_ROPE // 2
    return jnp.concatenate([-w[..., half:], w[..., :half]], axis=-1)


def _even_proj(x, S, w_in, q_norm, w_uq, kv_norm, w_ukv, *, tm_pref=256):
    T = x.shape[0]
    tm = _tile(S, tm_pref)
    nt_seq = S // tm
    H = MLA_HEADS

    wq = w_in[:, :Q_LORA]
    wkv = w_in[:, Q_LORA:Q_LORA + KV_LORA]
    wkr = w_in[:, Q_LORA + KV_LORA:MLA_IN]
    zk = jnp.zeros((D_MODEL, MLA_NOPE), F32)
    zt = jnp.zeros((D_MODEL, LANES - MLA_QK), F32)
    wkra = jnp.concatenate([zk, wkr, zt], axis=1)
    wkrb = jnp.concatenate([zk, _rope_swap(wkr), zt], axis=1)
    wrw = w_in[:, MLA_IN:]
    wrkv = wrw[:, :3 * RW_WIDTH]
    wlora = jnp.pad(wrw[:, 3 * RW_WIDTH:], ((0, 0), (0, RW_LORA_PAD - RW_LORA)))

    uq = w_uq.reshape(Q_LORA, H, MLA_QK)
    pad_q = jnp.zeros((Q_LORA, H, LANES - MLA_QK), F32)
    wuq = jnp.concatenate([uq, pad_q], axis=-1).reshape(Q_LORA, H * LANES)
    wuqs = jnp.concatenate([jnp.zeros((Q_LORA, H, MLA_NOPE), F32), _rope_swap(uq[..., MLA_NOPE:]), pad_q],
                           axis=-1).reshape(Q_LORA, H * LANES)
    ukv = w_ukv.reshape(KV_LORA, H, MLA_NOPE + MLA_V)
    wuk = jnp.concatenate([ukv[..., :MLA_NOPE], jnp.zeros((KV_LORA, H, LANES - MLA_NOPE), F32)],
                          axis=-1).reshape(KV_LORA, H * LANES)
    wuv = jnp.concatenate([ukv[..., MLA_NOPE:], jnp.zeros((KV_LORA, H, LANES - MLA_V), F32)],
                          axis=-1).reshape(KV_LORA, H * LANES)

    half = MLA_ROPE // 2
    inv = ROPE_THETA ** (-jnp.arange(half, dtype=F32) / half)
    ang = jnp.arange(S, dtype=F32)[:, None] * inv[None, :]
    zl = jnp.zeros((S, MLA_NOPE), F32)
    zr = jnp.zeros((S, LANES - MLA_QK), F32)
    cos_t = jnp.concatenate([zl, jnp.cos(ang), jnp.cos(ang), zr], axis=1)
    sin_t = jnp.concatenate([zl, jnp.sin(ang), jnp.sin(ang), zr], axis=1)

    def const(shape):
        return pl.BlockSpec(shape, lambda i: (0,) * len(shape))

    bw = lambda a: a.astype(BF16)
    outs = pl.pallas_call(
        _even_proj_kernel,
        grid=(T // tm,),
        in_specs=[
            pl.BlockSpec((tm, D_MODEL), lambda i: (i, 0)),
            const((D_MODEL, Q_LORA)), const((D_MODEL, KV_LORA)),
            const((D_MODEL, LANES)), const((D_MODEL, LANES)),
            const((D_MODEL, 3 * RW_WIDTH)), const((D_MODEL, RW_LORA_PAD)),
            const((1, Q_LORA)), const((1, KV_LORA)),
            const((Q_LORA, H * LANES)), const((Q_LORA, H * LANES)),
            const((KV_LORA, H * LANES)), const((KV_LORA, H * LANES)),
            pl.BlockSpec((tm, LANES), lambda i: (i % nt_seq, 0)),
            pl.BlockSpec((tm, LANES), lambda i: (i % nt_seq, 0)),
        ],
        out_specs=[
            pl.BlockSpec((H, tm, LANES), lambda i: (0, i, 0)),
            pl.BlockSpec((H, tm, LANES), lambda i: (0, i, 0)),
            pl.BlockSpec((H, tm, LANES), lambda i: (0, i, 0)),
            pl.BlockSpec((tm, 3 * RW_WIDTH), lambda i: (i, 0)),
            pl.BlockSpec((tm, RW_LORA_PAD), lambda i: (i, 0)),
        ],
        out_shape=[
            jax.ShapeDtypeStruct((H, T, LANES), BF16),
            jax.ShapeDtypeStruct((H, T, LANES), BF16),
            jax.ShapeDtypeStruct((H, T, LANES), BF16),
            jax.ShapeDtypeStruct((T, 3 * RW_WIDTH), F32),
            jax.ShapeDtypeStruct((T, RW_LORA_PAD), F32),
        ],
        compiler_params=_cparams(("parallel",)),
        name="even_proj",
    )(x, bw(wq), bw(wkv), bw(wkra), bw(wkrb), bw(wrkv), bw(wlora),
      q_norm.reshape(1, -1), kv_norm.reshape(1, -1),
      bw(wuq), bw(wuqs), bw(wuk), bw(wuv), cos_t, sin_t)
    return outs


def _attn_kernel(q_ref, k_ref, v_ref, o_ref, *, n_kv, tk):
    qa = q_ref[0]
    qb = q_ref[1]
    tq = qa.shape[0]

    def one_head(q, kk, vv, m, acc):
        s = lax.dot_general(q, kk, _NT, preferred_element_type=F32)
        m_new = jnp.maximum(m, jnp.max(s, axis=-1, keepdims=True))
        alpha = jnp.exp2(m - m_new)
        p = jnp.exp2((s - m_new).astype(BF16))
        acc = alpha * acc + jnp.dot(p, vv, preferred_element_type=F32)
        return m_new, acc

    def body(c, carry):
        ma, acca, mb, accb = carry
        rows = pl.ds(pl.multiple_of(c * tk, tk), tk)
        ma, acca = one_head(qa, k_ref[0, rows, :], v_ref[0, rows, :], ma, acca)
        mb, accb = one_head(qb, k_ref[1, rows, :], v_ref[1, rows, :], mb, accb)
        return ma, acca, mb, accb

    m0 = jnp.full((tq, 1), -jnp.inf, F32)
    a0 = jnp.zeros((tq, LANES), F32)
    _, acca, _, accb = lax.fori_loop(0, n_kv, body, (m0, a0, m0, a0), unroll=2)
    lane = lax.broadcasted_iota(jnp.int32, (tq, LANES), 1)
    oa = acca / acca[:, MLA_V:MLA_V + 1]
    ob = pltpu.roll(accb / accb[:, MLA_V:MLA_V + 1], MLA_V, axis=1)
    o_ref[...] = jnp.where(lane < MLA_V, oa, ob).astype(o_ref.dtype)


def _attention(q, k, v, B, S, *, tq_pref=1024, tk_pref=512):
    H, T, _ = q.shape
    tq = _tile(S, tq_pref)
    tk = _tile(S, tk_pref)
    nq = S // tq
    return pl.pallas_call(
        functools.partial(_attn_kernel, n_kv=S // tk, tk=tk),
        grid=(B, H // 2, nq),
        in_specs=[
            pl.BlockSpec((2, tq, LANES), lambda b, p, i: (p, b * nq + i, 0)),
            pl.BlockSpec((2, S, LANES), lambda b, p, i: (p, b, 0)),
            pl.BlockSpec((2, S, LANES), lambda b, p, i: (p, b, 0)),
        ],
        out_specs=pl.BlockSpec((tq, LANES), lambda b, p, i: (b * nq + i, p)),
        out_shape=jax.ShapeDtypeStruct((T, H * MLA_V), BF16),
        compiler_params=_cparams(("parallel", "parallel", "arbitrary")),
        name="mla_attention",
    )(q, k, v)


def _shifted(main_ref, prev_ref, next_ref, ext_ref, first, last, reach):
    tm = main_ref.shape[0]
    ext_ref[0:HALO, :] = jnp.where(first, 0.0, prev_ref[...])
    ext_ref[HALO:HALO + tm, :] = main_ref[...]
    ext_ref[HALO + tm:2 * HALO + tm, :] = jnp.where(last, 0.0, next_ref[...])
    del reach

    def load(d):
        return ext_ref[HALO + d:HALO + d + tm, :]

    return load


def _head_sum(x, ones_ref):
    return _dot_exact_rhs(x, ones_ref[...])


def _rwkv_prep_kernel(rkv_ref, rkvp_ref, rkvn_ref, lo_ref, lop_ref, lon_ref,
                      mu_rkv_ref, mu_lo_ref, wl_ref, bias_ref, vec_ref, ones_ref,
                      r_ref, k_ref, v_ref, kk_ref, kka_ref, lw_ref, bonus_ref, g_ref,
                      ext_rkv, ext_lo, *, nt_seq):
    i = pl.program_id(0)
    first = (i % nt_seq) == 0
    last = (i % nt_seq) == nt_seq - 1
    W = RW_WIDTH

    ld = _shifted(rkv_ref, rkvp_ref, rkvn_ref, ext_rkv, first, last, 1)
    cur = rkv_ref[...]
    rkv = cur + (0.5 * (ld(-1) + ld(1)) - cur) * mu_rkv_ref[...]
    ld2 = _shifted(lo_ref, lop_ref, lon_ref, ext_lo, first, last, 1)
    cur2 = lo_ref[...]
    lo = cur2 + (0.5 * (ld2(-1) + ld2(1)) - cur2) * mu_lo_ref[...]

    r = rkv[:, 0:W]
    k = rkv[:, W:2 * W]
    v = rkv[:, 2 * W:3 * W]

    lane = lax.broadcasted_iota(jnp.int32, lo.shape, 1)
    act = jnp.where(lane < 2 * DECAY_LORA, jnp.tanh(lo),
                    jnp.where(lane < 2 * DECAY_LORA + AAA_LORA, lo, jax.nn.sigmoid(lo)))
    proj = jnp.dot(act.astype(BF16), wl_ref[...], preferred_element_type=F32) + bias_ref[...]
    u_f = proj[:, 0:W]
    u_b = proj[:, W:2 * W]
    a = jax.nn.sigmoid(proj[:, 2 * W:3 * W])
    g = proj[:, 3 * W:4 * W]

    k_k = vec_ref[0:1, :]
    k_a = vec_ref[1:2, :]
    r_k = vec_ref[2:3, :]

    c = math.exp(-0.5)
    lw_ref[0] = -c * jax.nn.sigmoid(u_f)
    lw_ref[1] = -c * jax.nn.sigmoid(u_b)

    kx = k * k_k
    kk = kx * lax.rsqrt(_head_sum(kx * kx, ones_ref) + 1e-12)
    k2 = k * (1.0 + (a - 1.0) * k_a)

    r_ref[...] = r
    k_ref[...] = k2
    v_ref[...] = v
    kk_ref[...] = kk
    kka_ref[...] = kk * a
    bonus_ref[...] = _head_sum(r * k2 * r_k, ones_ref) * v
    g_ref[...] = g


def _rwkv_prep(rkv, lora, S, mu, w0, w2, a0, a2, g2, k_k, k_a, r_k, ones_blk, *, tm_pref=256):
    T = rkv.shape[0]
    tm = _tile(S, tm_pref)
    nt_seq = S // tm
    W = RW_WIDTH
    hb = tm // HALO
    last_hb = T // HALO - 1

    mu_rkv = mu[:3 * W].reshape(1, -1)
    mu_lo = jnp.pad(mu[3 * W:], (0, RW_LORA_PAD - RW_LORA)).reshape(1, -1)
    wl = jnp.zeros((RW_LORA_PAD, 4 * W), F32)
    o1, o2, o3 = DECAY_LORA, 2 * DECAY_LORA, 2 * DECAY_LORA + AAA_LORA
    wl = wl.at[0:o1, 0:W].set(w2[0])
    wl = wl.at[o1:o2, W:2 * W].set(w2[1])
    wl = wl.at[o2:o3, 2 * W:3 * W].set(a2)
    wl = wl.at[o3:o3 + GATE_LORA, 3 * W:4 * W].set(g2)
    bias = jnp.concatenate([w0[0], w0[1], a0, jnp.zeros((W,), F32)]).reshape(1, -1)
    vecs = jnp.concatenate([k_k.reshape(1, W), k_a.reshape(1, W), r_k.reshape(1, W),
                            jnp.zeros((SUBLANES - 3, W), F32)], axis=0)

    def tiled(c):
        return pl.BlockSpec((tm, c), lambda i: (i, 0))

    def prev(c):
        return pl.BlockSpec((HALO, c), lambda i: (jnp.maximum(i * hb - 1, 0), 0))

    def nxt(c):
        return pl.BlockSpec((HALO, c), lambda i: (jnp.minimum((i + 1) * hb, last_hb), 0))

    def const(shape):
        return pl.BlockSpec(shape, lambda i: (0,) * len(shape))

    o512 = jax.ShapeDtypeStruct((T, W), F32)
    return pl.pallas_call(
        functools.partial(_rwkv_prep_kernel, nt_seq=nt_seq),
        grid=(T // tm,),
        in_specs=[tiled(3 * W), prev(3 * W), nxt(3 * W),
                  tiled(RW_LORA_PAD), prev(RW_LORA_PAD), nxt(RW_LORA_PAD),
                  const((1, 3 * W)), const((1, RW_LORA_PAD)),
                  const((RW_LORA_PAD, 4 * W)), const((1, 4 * W)),
                  const((SUBLANES, W)), const((W, W))],
        out_specs=[tiled(W), tiled(W), tiled(W), tiled(W), tiled(W),
                   pl.BlockSpec((2, tm, W), lambda i: (0, i, 0)),
                   tiled(W), tiled(W)],
        out_shape=[o512, o512, o512, o512, o512,
                   jax.ShapeDtypeStruct((2, T, W), F32), o512, o512],
        scratch_shapes=[pltpu.VMEM((tm + 2 * HALO, 3 * W), F32),
                        pltpu.VMEM((tm + 2 * HALO, RW_LORA_PAD), F32)],
        compiler_params=_cparams(("parallel",)),
        name="rwkv_prep",
    )(rkv, rkv, rkv, lora, lora, lora, mu_rkv, mu_lo, wl.astype(BF16), bias, vecs, ones_blk)


def _b(a):
    return a.astype(BF16)


def _unit_tri_inverse_all(Ms, blk_masks, eye):
    n = Ms[0].shape[0]
    Md = [jnp.where(blk_masks[RW_INV_BLOCK], M, 0.0) for M in Ms]
    Mdb = [_b(M) for M in Md]
    M2 = [_dg(a, a, _NN) for a in Mdb]
    M2b = [_b(a) for a in M2]
    M4 = [_dg(a, a, _NN) for a in M2b]
    M4b = [_b(a) for a in M4]
    M8 = [_dg(a, a, _NN) for a in M4b]
    T = [_dg(_b(eye - a), _b(eye + c), _NN) for a, c in zip(Md, M2)]
    T = [_dg(_b(t), _b(eye + c), _NN) for t, c in zip(T, M4)]
    T = [_dg(_b(t), _b(eye + c), _NN) for t, c in zip(T, M8)]
    s = RW_INV_BLOCK
    while s < n:
        off = jnp.logical_and(blk_masks[2 * s], jnp.logical_not(blk_masks[s]))
        Tb = [_b(t) for t in T]
        TC = [_dg(tb, _b(jnp.where(off, M, 0.0)), _NN) for tb, M in zip(Tb, Ms)]
        T = [t - _dg(_b(tc), tb, _NN) for t, tc, tb in zip(T, TC, Tb)]
        s *= 2
    return T


def _rwkv_scan_kernel(r_ref, k_ref, v_ref, kk_ref, kka_ref, lw_ref, y_ref, s_ref, *, n_sub):
    d = pl.program_id(0)
    c = pl.program_id(2)
    L = RW_CHUNK
    N = RW_HEAD
    H = RW_HEADS
    fwd = d == 0

    @pl.when(c == 0)
    def _():
        s_ref[...] = jnp.zeros_like(s_ref)

    row = lax.broadcasted_iota(jnp.int32, (L, L), 0)
    col = lax.broadcasted_iota(jnp.int32, (L, L), 1)
    ahead = (row - col) * (1 - 2 * d)
    earlier = ahead > 0
    upto = ahead >= 0
    cum_mask = jnp.where(upto, 1.0, 0.0).astype(BF16)
    eye = jnp.where(row == col, 1.0, 0.0).astype(F32)
    blk_masks = {}
    s = RW_INV_BLOCK
    while s <= L:
        blk_masks[s] = (row // s) == (col // s)
        s *= 2

    rows_of, w_tot = [], []
    kap, Rh, Ah, Kh, Vs = [], [], [], [], []
    for sidx in range(n_sub):
        j = jnp.where(fwd, sidx, n_sub - 1 - sidx)
        rows = pl.ds(pl.multiple_of(j * L, L), L)
        rows_of.append(rows)
        lw = lw_ref[0, rows, :]
        cs = _dot_exact_lhs(cum_mask, lw)
        e_neg = jnp.exp(-cs)
        ah = kka_ref[rows, :] * e_neg
        kh = k_ref[rows, :] * e_neg
        kp = kk_ref[rows, :] * jnp.exp(cs - lw)
        rh = r_ref[rows, :] * jnp.exp(cs)
        vv = v_ref[rows, :]
        w_tot.append(jnp.exp(jnp.where(fwd, cs[L - 1:L, :], cs[0:1, :])))
        for h in range(H):
            sl = slice(h * N, (h + 1) * N)
            kap.append(kp[:, sl])
            Rh.append(rh[:, sl])
            Ah.append(ah[:, sl])
            Kh.append(kh[:, sl])
            Vs.append(vv[:, sl])

    Vb = [_b(v) for v in Vs]
    Zb = [_b(jnp.concatenate([a, k], axis=0)) for a, k in zip(Ah, Kh)]
    G = [_dg(_b(jnp.concatenate([kp, rh], axis=0)), z, _NT) for kp, rh, z in zip(kap, Rh, Zb)]
    Mak = [jnp.where(earlier, g[0:L, 0:L], 0.0) for g in G]
    Mkk = [_b(jnp.where(earlier, g[0:L, L:2 * L], 0.0)) for g in G]
    Mar = [_b(jnp.where(upto, g[L:2 * L, 0:L], 0.0)) for g in G]
    Mkr = [_b(jnp.where(upto, g[L:2 * L, L:2 * L], 0.0)) for g in G]
    Tinv = _unit_tri_inverse_all(Mak, blk_masks, eye)
    MV = [_dg(m, v, _NN) for m, v in zip(Mkk, Vb)]
    A12 = [_dg(_b(t), _b(jnp.concatenate([kp, mv], axis=1)), _NN) for t, kp, mv in zip(Tinv, kap, MV)]
    MA = [_dg(m, _b(a), _NN) for m, a in zip(Mar, A12)]
    Q1 = [_b(rh - ma[:, 0:N]) for rh, ma in zip(Rh, MA)]
    Y0 = [_dg(m, v, _NN) - ma[:, N:2 * N] for m, v, ma in zip(Mkr, Vb, MA)]
    Pm = [_b(eye - _dg(_b(a[:, 0:N].T), _b(ah), _NN)) for a, ah in zip(A12, Ah)]
    Q2 = [_dg(_b(jnp.concatenate([v, -a[:, N:2 * N]], axis=0).T),
              _b(jnp.concatenate([kh, ah], axis=0)), _NN)
          for v, a, kh, ah in zip(Vs, A12, Kh, Ah)]

    S = [s_ref[h] for h in range(H)]
    for sidx in range(n_sub):
        ys = []
        for h in range(H):
            p = sidx * H + h
            s_hi, s_lo = _split2(S[h])
            ys.append(_dg(Q1[p], s_hi, _NT) + _dg(Q1[p], s_lo, _NT) + Y0[p])
            S[h] = ((_dg(s_hi, Pm[p], _NN) + _dg(s_lo, Pm[p], _NN) + Q2[p])
                    * w_tot[sidx][:, h * N:(h + 1) * N])
        y_ref[0, rows_of[sidx], :] = jnp.concatenate(ys, axis=1)
    for h in range(H):
        s_ref[h] = S[h]


def _rwkv_scan(r, k, v, kk, kka, lw, B, S, *, tb_pref=256):
    T, W = r.shape
    tb = _tile(S, tb_pref)
    nb = S // tb

    def tmap(d, b, c):
        return (b * nb + jnp.where(d == 0, c, nb - 1 - c), 0)

    tiled = pl.BlockSpec((tb, W), tmap)
    dir_tiled = pl.BlockSpec((1, tb, W), lambda d, b, c: (d,) + tmap(d, b, c))
    return pl.pallas_call(
        functools.partial(_rwkv_scan_kernel, n_sub=tb // RW_CHUNK),
        grid=(2, B, nb),
        in_specs=[tiled, tiled, tiled, tiled, tiled, dir_tiled],
        out_specs=dir_tiled,
        out_shape=jax.ShapeDtypeStruct((2, T, W), F32),
        scratch_shapes=[pltpu.VMEM((RW_HEADS, RW_HEAD, RW_HEAD), F32)],
        compiler_params=_cparams(("parallel", "parallel", "arbitrary")),
        name="rwkv_scan",
    )(r, k, v, kk, kka, lw)


def _even_out_kernel(x_ref, oa_ref, y_ref, bonus_ref, g_ref, gn_ref, ones_ref,
                     wa_ref, wb_ref, lg_ref, lb_ref, o_ref):
    y = y_ref[0] + y_ref[1]
    inv_n = 1.0 / RW_HEAD
    mean = _head_sum(y, ones_ref) * inv_n
    dy = y - mean
    var = _head_sum(dy * dy, ones_ref) * inv_n
    yn = dy * lax.rsqrt(var + RW_GN_EPS) * gn_ref[0:1, :] + gn_ref[1:2, :]
    ob = ((yn + bonus_ref[...]) * g_ref[...]).astype(BF16)
    m = (jnp.dot(oa_ref[...], wa_ref[...], preferred_element_type=F32)
         + jnp.dot(ob, wb_ref[...], preferred_element_type=F32))
    o_ref[...] = _ln(DN_ALPHA * x_ref[...] + m, lg_ref[...], lb_ref[...])


def _even_out(x, oa, y, bonus, g, gn_g, gn_b, ones_blk, w_out, ln_g, ln_b, *, tm_pref=256):
    T = x.shape[0]
    tm = _tile(T, tm_pref)
    W = RW_WIDTH
    gn = jnp.concatenate([gn_g.reshape(1, W), gn_b.reshape(1, W), jnp.zeros((SUBLANES - 2, W), F32)], axis=0)
    wa = w_out[:MLA_HEADS * MLA_V].astype(BF16)
    wb = w_out[MLA_HEADS * MLA_V:].astype(BF16)

    def tiled(c):
        return pl.BlockSpec((tm, c), lambda i: (i, 0))

    def const(shape):
        return pl.BlockSpec(shape, lambda i: (0,) * len(shape))

    return pl.pallas_call(
        _even_out_kernel,
        grid=(T // tm,),
        in_specs=[tiled(D_MODEL), tiled(MLA_HEADS * MLA_V),
                  pl.BlockSpec((2, tm, W), lambda i: (0, i, 0)),
                  tiled(W), tiled(W), const((SUBLANES, W)), const((W, W)),
                  const((MLA_HEADS * MLA_V, D_MODEL)), const((W, D_MODEL)),
                  const((1, D_MODEL)), const((1, D_MODEL))],
        out_specs=tiled(D_MODEL),
        out_shape=jax.ShapeDtypeStruct((T, D_MODEL), F32),
        compiler_params=_cparams(("parallel",)),
        name="even_out",
    )(x, oa, y, bonus, g, gn, ones_blk, wa, wb, ln_g.reshape(1, -1), ln_b.reshape(1, -1))


def _matmul_kernel(x_ref, w_ref, o_ref):
    o_ref[...] = jnp.dot(x_ref[...].astype(BF16), w_ref[...], preferred_element_type=F32).astype(o_ref.dtype)


def _matmul(x, w, *, tm_pref=512, tn_pref=1024, out_dtype=F32, name="matmul"):
    T, K = x.shape
    N = w.shape[1]
    tm = _tile(T, tm_pref)
    tn = N if N <= tn_pref else tn_pref
    assert N % tn == 0
    return pl.pallas_call(
        _matmul_kernel,
        grid=(T // tm, N // tn),
        in_specs=[pl.BlockSpec((tm, K), lambda i, j: (i, 0)),
                  pl.BlockSpec((K, tn), lambda i, j: (0, j))],
        out_specs=pl.BlockSpec((tm, tn), lambda i, j: (i, j)),
        out_shape=jax.ShapeDtypeStruct((T, N), out_dtype),
        compiler_params=_cparams(("parallel", "arbitrary")),
        name=name,
    )(x, w.astype(BF16))


def _conv_kernel(u_ref, up_ref, un_ref, w_ref, b_ref, xs_ref, bm_ref, cm_ref, ext_ref, *, nt_seq):
    i = pl.program_id(0)
    first = (i % nt_seq) == 0
    last = (i % nt_seq) == nt_seq - 1
    ld = _shifted(u_ref, up_ref, un_ref, ext_ref, first, last, M_CONV // 2)
    acc = b_ref[...] + ld(-2) * w_ref[0:1, :]
    for j in range(1, M_CONV):
        acc = acc + ld(j - M_CONV // 2) * w_ref[j:j + 1, :]
    y = acc * jax.nn.sigmoid(acc)
    xs_ref[...] = y[:, 0:M_INNER]
    bm_ref[...] = y[:, M_INNER:M_INNER + M_GROUPS * M_STATE]
    cm_ref[...] = y[:, M_INNER + M_GROUPS * M_STATE:]


def _conv_silu(xbc, S, conv_w, conv_b, *, tm_pref=256):
    T, C = xbc.shape
    tm = _tile(S, tm_pref)
    nt_seq = S // tm
    hb = tm // HALO
    last_hb = T // HALO - 1
    wpad = jnp.concatenate([conv_w, jnp.zeros((SUBLANES - M_CONV, C), F32)], axis=0)
    GN = M_GROUPS * M_STATE
    return pl.pallas_call(
        functools.partial(_conv_kernel, nt_seq=nt_seq),
        grid=(T // tm,),
        in_specs=[pl.BlockSpec((tm, C), lambda i: (i, 0)),
                  pl.BlockSpec((HALO, C), lambda i: (jnp.maximum(i * hb - 1, 0), 0)),
                  pl.BlockSpec((HALO, C), lambda i: (jnp.minimum((i + 1) * hb, last_hb), 0)),
                  pl.BlockSpec((SUBLANES, C), lambda i: (0, 0)),
                  pl.BlockSpec((1, C), lambda i: (0, 0))],
        out_specs=[pl.BlockSpec((tm, M_INNER), lambda i: (i, 0)),
                   pl.BlockSpec((tm, GN), lambda i: (i, 0)),
                   pl.BlockSpec((tm, GN), lambda i: (i, 0))],
        out_shape=[jax.ShapeDtypeStruct((T, M_INNER), F32),
                   jax.ShapeDtypeStruct((T, GN), F32),
                   jax.ShapeDtypeStruct((T, GN), F32)],
        scratch_shapes=[pltpu.VMEM((tm + 2 * HALO, C), F32)],
        compiler_params=_cparams(("parallel",)),
        name="conv_silu",
    )(xbc, xbc, xbc, wpad, conv_b.reshape(1, -1))


def _ssd_kernel(xs_ref, bm_ref, cm_ref, dt_ref, dtb_ref, alog_ref, exp_ref, y_ref, st_ref):
    d = pl.program_id(0)
    c = pl.program_id(2)
    L = SSD_CHUNK
    P = M_HEADDIM
    GW = M_GROUP_W
    fwd = d == 0

    @pl.when(c == 0)
    def _():
        st_ref[...] = jnp.zeros_like(st_ref)

    row = lax.broadcasted_iota(jnp.int32, (L, L), 0)
    col = lax.broadcasted_iota(jnp.int32, (L, L), 1)
    upto = (row - col) * (1 - 2 * d) >= 0
    cum_mask = jnp.where(upto, 1.0, 0.0).astype(BF16)

    z = dt_ref[...] + dtb_ref[...]
    dt = jnp.maximum(z, 0.0) + jnp.log1p(jnp.exp(-jnp.abs(z)))
    a = dt * (-jnp.exp(alog_ref[...]))
    cs = _dot_exact_lhs(cum_mask, a)
    csT = cs.T
    tot = jnp.where(fwd, cs[L - 1:L, :], cs[0:1, :])

    expand = exp_ref[0]
    dt_x = _dot_exact_rhs(dt, expand)
    cs_x = _dot_exact_rhs(cs, expand)
    tot_x = _dot_exact_rhs(jnp.broadcast_to(tot, (SUBLANES, LANES)), expand)[0:1, :]

    xd = xs_ref[...] * dt_x
    xd_b = xd.astype(BF16)
    into_state = (xd * jnp.exp(tot_x - cs_x)).astype(BF16)
    from_state = jnp.exp(cs_x)

    ys = []
    for g in range(M_GROUPS):
        gs = slice(g * GW, (g + 1) * GW)
        ns = slice(g * M_STATE, (g + 1) * M_STATE)
        Bg = bm_ref[:, ns]
        Cg = cm_ref[:, ns].astype(BF16)
        CB = lax.dot_general(Cg, Bg.astype(BF16), _NT, preferred_element_type=F32)
        st = st_ref[g]
        y_off = jnp.dot(Cg, st.astype(BF16), preferred_element_type=F32) * from_state[:, gs]
        y_heads = []
        for hh in range(M_HPG):
            hidx = g * M_HPG + hh
            lane_f = hidx
            lane_b = M_HEADS + hidx
            col_l = jnp.where(fwd, cs[:, lane_f:lane_f + 1], cs[:, lane_b:lane_b + 1])
            row_s = jnp.where(fwd, csT[lane_f:lane_f + 1, :], csT[lane_b:lane_b + 1, :])
            dec = jnp.exp(jnp.where(upto, col_l - row_s, -jnp.inf))
            Mh = (CB * dec).astype(BF16)
            hs = slice(g * GW + hh * P, g * GW + (hh + 1) * P)
            y_heads.append(jnp.dot(Mh, xd_b[:, hs], preferred_element_type=F32))
        ys.append(jnp.concatenate(y_heads, axis=1) + y_off)
        st_ref[g] = st * jnp.exp(tot_x[:, gs]) + jnp.dot(Bg.T.astype(BF16), into_state[:, gs],
                                                          preferred_element_type=F32)
    y_ref[0] = jnp.concatenate(ys, axis=1)


def _ssd(xs, bm, cm, dt_raw, dt_bias, a_log, B, S):
    T = xs.shape[0]
    L = SSD_CHUNK
    nc = S // L
    GN = M_GROUPS * M_STATE

    def pad_lanes(v):
        return jnp.pad(v.reshape(1, -1), ((0, 0), (0, LANES - v.size)))

    heads = jnp.arange(M_INNER) // M_HEADDIM
    expand = jnp.stack([
        (jnp.arange(LANES)[:, None] == heads[None, :]),
        (jnp.arange(LANES)[:, None] == (heads[None, :] + M_HEADS)),
    ]).astype(BF16)

    def tmap(d, b, c):
        return (b * nc + jnp.where(d == 0, c, nc - 1 - c), 0)

    def tiled(w):
        return pl.BlockSpec((L, w), tmap)

    return pl.pallas_call(
        _ssd_kernel,
        grid=(2, B, nc),
        in_specs=[tiled(M_INNER), tiled(GN), tiled(GN), tiled(LANES),
                  pl.BlockSpec((1, LANES), lambda d, b, c: (0, 0)),
                  pl.BlockSpec((1, LANES), lambda d, b, c: (0, 0)),
                  pl.BlockSpec((1, LANES, M_INNER), lambda d, b, c: (d, 0, 0))],
        out_specs=pl.BlockSpec((1, L, M_INNER), lambda d, b, c: (d,) + tmap(d, b, c)),
        out_shape=jax.ShapeDtypeStruct((2, T, M_INNER), F32),
        scratch_shapes=[pltpu.VMEM((M_GROUPS, M_STATE, M_GROUP_W), F32)],
        compiler_params=_cparams(("parallel", "parallel", "arbitrary")),
        name="ssd_scan",
    )(xs, bm, cm, dt_raw, pad_lanes(dt_bias), pad_lanes(a_log), expand)


def _odd_out_kernel(x_ref, y_ref, xs_ref, z_ref, dvec_ref, ng_ref, w_ref, lg_ref, lb_ref, o_ref):
    z = z_ref[...]
    y = (y_ref[0] + y_ref[1] + xs_ref[...] * dvec_ref[...]) * (z * jax.nn.sigmoid(z))
    parts = []
    for g in range(M_GROUPS):
        yg = y[:, g * M_GROUP_W:(g + 1) * M_GROUP_W]
        parts.append(yg * lax.rsqrt(jnp.mean(yg * yg, axis=-1, keepdims=True) + RMS_EPS))
    yn = (jnp.concatenate(parts, axis=1) * ng_ref[...]).astype(BF16)
    m = jnp.dot(yn, w_ref[...], preferred_element_type=F32)
    o_ref[...] = _ln(DN_ALPHA * x_ref[...] + m, lg_ref[...], lb_ref[...])


def _odd_out(x, y, xs, z, d_skip, norm_g, w_out, ln_g, ln_b, *, tm_pref=256):
    T = x.shape[0]
    tm = _tile(T, tm_pref)
    dvec = jnp.repeat(d_skip, M_HEADDIM).reshape(1, -1)

    def tiled(c):
        return pl.BlockSpec((tm, c), lambda i: (i, 0))

    def const(shape):
        return pl.BlockSpec(shape, lambda i: (0,) * len(shape))

    return pl.pallas_call(
        _odd_out_kernel,
        grid=(T // tm,),
        in_specs=[tiled(D_MODEL), pl.BlockSpec((2, tm, M_INNER), lambda i: (0, i, 0)),
                  tiled(M_INNER), tiled(M_INNER), const((1, M_INNER)), const((1, M_INNER)),
                  const((M_INNER, D_MODEL)), const((1, D_MODEL)), const((1, D_MODEL))],
        out_specs=tiled(D_MODEL),
        out_shape=jax.ShapeDtypeStruct((T, D_MODEL), F32),
        compiler_params=_cparams(("parallel",)),
        name="odd_out",
    )(x, y, xs, z, dvec, norm_g.reshape(1, -1), w_out.astype(BF16),
      ln_g.reshape(1, -1), ln_b.reshape(1, -1))


def _even_mixer_ln(x, B, S, j, P, ln_g, ln_b, ones_blk):
    q, k, v, rkv, lora = _even_proj(x, S, P['w_in_even'][j], P['mla_q_norm'][j], P['mla_w_uq'][j],
                                    P['mla_kv_norm'][j], P['mla_w_ukv'][j])
    oa = _attention(q, k, v, B, S)
    r, k2, vv, kk, kka, lw, bonus, g = _rwkv_prep(
        rkv, lora, S, P['rw_mu'][j], P['rw_w0'][j], P['rw_w2'][j], P['rw_a0'][j], P['rw_a2'][j],
        P['rw_g2'][j], P['rw_k_k'][j], P['rw_k_a'][j], P['rw_r_k'][j], ones_blk)
    y = _rwkv_scan(r, k2, vv, kk, kka, lw, B, S)
    return _even_out(x, oa, y, bonus, g, P['rw_gn_g'][j], P['rw_gn_b'][j], ones_blk,
                     P['w_out_even'][j], ln_g, ln_b)


def _odd_mixer_ln(x, B, S, j, P, ln_g, ln_b):
    w_in = P['w_in_odd'][j]
    z = _matmul(x, w_in[:, :M_INNER], name="odd_proj_z")
    xbc = _matmul(x, w_in[:, M_INNER:M_INNER + M_CONV_DIM], name="odd_proj_xbc")
    w_dt = jnp.pad(w_in[:, M_INNER + M_CONV_DIM:], ((0, 0), (0, LANES - 2 * M_HEADS)))
    dt_raw = _matmul(x, w_dt, name="odd_proj_dt")
    xs, bm, cm = _conv_silu(xbc, S, P['m_conv_w'][j], P['m_conv_b'][j])
    y = _ssd(xs, bm, cm, dt_raw, P['m_dt_bias'][j].reshape(-1), P['m_A_log'][j].reshape(-1), B, S)
    return _odd_out(x, y, xs, z, P['m_D'][j], P['m_norm_g'][j], P['w_out_odd'][j], ln_g, ln_b)


def _trunk(x3, P):
    B, S, D = x3.shape
    x = x3.reshape(B * S, D)
    head_id = jnp.arange(RW_WIDTH) // RW_HEAD
    ones_blk = (head_id[:, None] == head_id[None, :]).astype(BF16)
    for i in range(DEPTH):
        x = _ffn_ln(x, P['ffn1_in'][i], P['ffn1_out'][i], P['ln_g'][i, 0], P['ln_b'][i, 0])
        if i % 2 == 0:
            x = _even_mixer_ln(x, B, S, i // 2, P, P['ln_g'][i, 1], P['ln_b'][i, 1], ones_blk)
        else:
            x = _odd_mixer_ln(x, B, S, i // 2, P, P['ln_g'][i, 1], P['ln_b'][i, 1])
        x = _ffn_ln(x, P['ffn2_in'][i], P['ffn2_out'][i], P['ln_g'][i, 2], P['ln_b'][i, 2])
    return x.reshape(B, S, D)


def kernel(x_prompt, x_sample, ffn1_in, ffn1_out, ffn2_in, ffn2_out, ln_g, ln_b, w_in_even, w_out_even, mla_q_norm, mla_w_uq, mla_kv_norm, mla_w_ukv, rw_mu, rw_w0, rw_w2, rw_a0, rw_a2, rw_g2, rw_k_k, rw_k_a, rw_r_k, rw_gn_g, rw_gn_b, w_in_odd, m_conv_w, m_conv_b, m_dt_bias, m_A_log, m_D, m_norm_g, w_out_odd):
    P = dict(ffn1_in=ffn1_in, ffn1_out=ffn1_out, ffn2_in=ffn2_in, ffn2_out=ffn2_out,
             ln_g=ln_g, ln_b=ln_b, w_in_even=w_in_even, w_out_even=w_out_even,
             mla_q_norm=mla_q_norm, mla_w_uq=mla_w_uq, mla_kv_norm=mla_kv_norm,
             mla_w_ukv=mla_w_ukv, rw_mu=rw_mu, rw_w0=rw_w0, rw_w2=rw_w2, rw_a0=rw_a0,
             rw_a2=rw_a2, rw_g2=rw_g2, rw_k_k=rw_k_k, rw_k_a=rw_k_a, rw_r_k=rw_r_k,
             rw_gn_g=rw_gn_g, rw_gn_b=rw_gn_b, w_in_odd=w_in_odd, m_conv_w=m_conv_w,
             m_conv_b=m_conv_b, m_dt_bias=m_dt_bias, m_A_log=m_A_log, m_D=m_D,
             m_norm_g=m_norm_g, w_out_odd=w_out_odd)
    return (_trunk(x_prompt, P), _trunk(x_sample, P))
```

```python
import functools
import math

import jax
import jax.numpy as jnp
from jax import lax
from jax.experimental import pallas as pl
from jax.experimental.pallas import tpu as pltpu

F32 = jnp.float32
BF16 = jnp.bfloat16

D_MODEL = 1024
DEPTH = 4
DN_ALPHA = (2.0 * DEPTH) ** 0.25
LN_EPS = 1e-5
RMS_EPS = 1e-6
D_FF = 2816

MLA_HEADS = 8
MLA_NOPE = 64
MLA_ROPE = 32
MLA_QK = MLA_NOPE + MLA_ROPE
MLA_V = 64
Q_LORA = 384
KV_LORA = 256
ROPE_THETA = 10000.0
MLA_SCALE = MLA_QK ** -0.5
MLA_IN = Q_LORA + KV_LORA + MLA_ROPE

RW_HEADS = 8
RW_HEAD = 64
RW_WIDTH = RW_HEADS * RW_HEAD
DECAY_LORA = 64
AAA_LORA = 64
GATE_LORA = 128
RW_GN_EPS = 64e-5
RW_LORA = 2 * DECAY_LORA + AAA_LORA + GATE_LORA
RW_LORA_PAD = 384

M_INNER = 2 * D_MODEL
M_HEADDIM = 64
M_HEADS = M_INNER // M_HEADDIM
M_GROUPS = 4
M_HPG = M_HEADS // M_GROUPS
M_STATE = 128
M_CONV = 5
SSD_CHUNK = 128
M_CONV_DIM = M_INNER + 2 * M_GROUPS * M_STATE
M_GROUP_W = M_INNER // M_GROUPS

LANES = 128
SUBLANES = 8
HALO = SUBLANES
VMEM_LIMIT = 56 * 1024 * 1024

RW_CHUNK = 64
RW_INV_BLOCK = 16


def _tile(n, pref):
    t = min(n, pref)
    while n % t:
        t -= SUBLANES
    return t


def _cparams(sem):
    return pltpu.CompilerParams(dimension_semantics=sem, vmem_limit_bytes=VMEM_LIMIT)


def _ln(y, g, b):
    mu = jnp.mean(y, axis=-1, keepdims=True)
    d = y - mu
    var = jnp.mean(d * d, axis=-1, keepdims=True)
    return d * lax.rsqrt(var + LN_EPS) * g + b


def _bdot(a, b):
    return jnp.dot(a.astype(BF16), b.astype(BF16), preferred_element_type=F32)


def _split3(a):
    hi = a.astype(BF16)
    r1 = a - hi.astype(F32)
    mid = r1.astype(BF16)
    lo = (r1 - mid.astype(F32)).astype(BF16)
    return hi, mid, lo


def _split2(a):
    hi = a.astype(BF16)
    lo = (a - hi.astype(F32)).astype(BF16)
    return hi, lo


_NN = (((1,), (0,)), ((), ()))
_NT = (((1,), (1,)), ((), ()))


def _dg(a, b, dims):
    return lax.dot_general(a, b, dims, preferred_element_type=F32)


def _dot3(a, b, dims=_NN):
    ah, al = _split2(a)
    bh, bl = _split2(b)
    return _dg(ah, bh, dims) + (_dg(ah, bl, dims) + _dg(al, bh, dims))


def _dot_exact_lhs(a_bf16, b):
    b0, b1, b2 = _split3(b)
    return _dg(a_bf16, b0, _NN) + (_dg(a_bf16, b1, _NN) + _dg(a_bf16, b2, _NN))


def _dot_exact_rhs(a, b_bf16):
    a0, a1, a2 = _split3(a)
    return _dg(a0, b_bf16, _NN) + (_dg(a1, b_bf16, _NN) + _dg(a2, b_bf16, _NN))


def _dot2_rhs(a, b_bf16):
    a0, a1 = _split2(a)
    return _dg(a0, b_bf16, _NN) + _dg(a1, b_bf16, _NN)


def _ffn_kernel(x_ref, wg_ref, wu_ref, wo_ref, g_ref, b_ref, o_ref):
    x = x_ref[...]
    xb = x.astype(BF16)
    gate = jnp.dot(xb, wg_ref[...], preferred_element_type=F32)
    up = jnp.dot(xb, wu_ref[...], preferred_element_type=F32)
    h = (gate * jax.nn.sigmoid(gate) * up).astype(BF16)
    m = jnp.dot(h, wo_ref[...], preferred_element_type=F32)
    o_ref[...] = _ln(DN_ALPHA * x + 0.5 * m, g_ref[...], b_ref[...])


def _resident(shape, index_map):
    return pl.BlockSpec(shape, index_map, pipeline_mode=pl.Buffered(1))


def _ffn_ln(x, w_in, w_out, g, b, *, tm_pref=512):
    T = x.shape[0]
    tm = _tile(T, tm_pref)
    w_in = w_in.astype(BF16)
    w_out = w_out.astype(BF16)
    return pl.pallas_call(
        _ffn_kernel,
        grid=(T // tm,),
        in_specs=[
            pl.BlockSpec((tm, D_MODEL), lambda i: (i, 0)),
            _resident((D_MODEL, D_FF), lambda i: (0, 0)),
            _resident((D_MODEL, D_FF), lambda i: (0, 1)),
            _resident((D_FF, D_MODEL), lambda i: (0, 0)),
            _resident((1, D_MODEL), lambda i: (0, 0)),
            _resident((1, D_MODEL), lambda i: (0, 0)),
        ],
        out_specs=pl.BlockSpec((tm, D_MODEL), lambda i: (i, 0)),
        out_shape=jax.ShapeDtypeStruct((T, D_MODEL), F32),
        compiler_params=_cparams(("parallel",)),
        name="ffn_ln",
    )(x, w_in, w_in, w_out, g.reshape(1, -1), b.reshape(1, -1))


def _even_proj_kernel(x_ref, wq_ref, wkv_ref, wkra_ref, wkrb_ref, wrkv_ref, wlora_ref,
                      qg_ref, kvg_ref, wuq_ref, wuqs_ref, wuk_ref, wuv_ref,
                      cos_ref, sin_ref,
                      q_ref, k_ref, v_ref, rkv_ref, lora_ref):
    xb = x_ref[...].astype(BF16)
    rkv_ref[...] = jnp.dot(xb, wrkv_ref[...], preferred_element_type=F32)
    lora_ref[...] = jnp.dot(xb, wlora_ref[...], preferred_element_type=F32)

    cq = jnp.dot(xb, wq_ref[...], preferred_element_type=F32)
    ckv = jnp.dot(xb, wkv_ref[...], preferred_element_type=F32)
    cqn = (cq * lax.rsqrt(jnp.mean(cq * cq, axis=-1, keepdims=True) + RMS_EPS) * qg_ref[...]).astype(BF16)
    ckvn = (ckv * lax.rsqrt(jnp.mean(ckv * ckv, axis=-1, keepdims=True) + RMS_EPS) * kvg_ref[...]).astype(BF16)

    cos = cos_ref[...]
    sin = sin_ref[...]
    lane = lax.broadcasted_iota(jnp.int32, cos.shape, 1)
    cos_q = jnp.where(lane < MLA_NOPE, 1.0, cos)

    kr = (jnp.dot(xb, wkra_ref[...], preferred_element_type=F32) * cos
          + jnp.dot(xb, wkrb_ref[...], preferred_element_type=F32) * sin)

    qa = jnp.dot(cqn, wuq_ref[...], preferred_element_type=F32)
    qs = jnp.dot(cqn, wuqs_ref[...], preferred_element_type=F32)
    kn = jnp.dot(ckvn, wuk_ref[...], preferred_element_type=F32)
    vv = jnp.dot(ckvn, wuv_ref[...], preferred_element_type=F32)
    q_scale = MLA_SCALE * math.log2(math.e)
    one_col = jnp.where(lane == MLA_V, 1.0, 0.0)
    for h in range(MLA_HEADS):
        sl = slice(h * LANES, (h + 1) * LANES)
        q_ref[h] = ((qa[:, sl] * cos_q + qs[:, sl] * sin) * q_scale).astype(BF16)
        k_ref[h] = (kn[:, sl] + kr).astype(BF16)
        v_ref[h] = (vv[:, sl] + one_col).astype(BF16)


def _rope_swap(w):
    half = MLA_ROPE // 2
    return jnp.concatenate([-w[..., half:], w[..., :half]], axis=-1)


def _even_proj(x, S, w_in, q_norm, w_uq, kv_norm, w_ukv, *, tm_pref=256):
    T = x.shape[0]
    tm = _tile(S, tm_pref)
    nt_seq = S // tm
    H = MLA_HEADS

    wq = w_in[:, :Q_LORA]
    wkv = w_in[:, Q_LORA:Q_LORA + KV_LORA]
    wkr = w_in[:, Q_LORA + KV_LORA:MLA_IN]
    zk = jnp.zeros((D_MODEL, MLA_NOPE), F32)
    zt = jnp.zeros((D_MODEL, LANES - MLA_QK), F32)
    wkra = jnp.concatenate([zk, wkr, zt], axis=1)
    wkrb = jnp.concatenate([zk, _rope_swap(wkr), zt], axis=1)
    wrw = w_in[:, MLA_IN:]
    wrkv = wrw[:, :3 * RW_WIDTH]
    wlora = jnp.pad(wrw[:, 3 * RW_WIDTH:], ((0, 0), (0, RW_LORA_PAD - RW_LORA)))

    uq = w_uq.reshape(Q_LORA, H, MLA_QK)
    pad_q = jnp.zeros((Q_LORA, H, LANES - MLA_QK), F32)
    wuq = jnp.concatenate([uq, pad_q], axis=-1).reshape(Q_LORA, H * LANES)
    wuqs = jnp.concatenate([jnp.zeros((Q_LORA, H, MLA_NOPE), F32), _rope_swap(uq[..., MLA_NOPE:]), pad_q],
                           axis=-1).reshape(Q_LORA, H * LANES)
    ukv = w_ukv.reshape(KV_LORA, H, MLA_NOPE + MLA_V)
    wuk = jnp.concatenate([ukv[..., :MLA_NOPE], jnp.zeros((KV_LORA, H, LANES - MLA_NOPE), F32)],
                          axis=-1).reshape(KV_LORA, H * LANES)
    wuv = jnp.concatenate([ukv[..., MLA_NOPE:], jnp.zeros((KV_LORA, H, LANES - MLA_V), F32)],
                          axis=-1).reshape(KV_LORA, H * LANES)

    half = MLA_ROPE // 2
    inv = ROPE_THETA ** (-jnp.arange(half, dtype=F32) / half)
    ang = jnp.arange(S, dtype=F32)[:, None] * inv[None, :]
    zl = jnp.zeros((S, MLA_NOPE), F32)
    zr = jnp.zeros((S, LANES - MLA_QK), F32)
    cos_t = jnp.concatenate([zl, jnp.cos(ang), jnp.cos(ang), zr], axis=1)
    sin_t = jnp.concatenate([zl, jnp.sin(ang), jnp.sin(ang), zr], axis=1)

    def const(shape):
        return pl.BlockSpec(shape, lambda i: (0,) * len(shape))

    bw = lambda a: a.astype(BF16)
    outs = pl.pallas_call(
        _even_proj_kernel,
        grid=(T // tm,),
        in_specs=[
            pl.BlockSpec((tm, D_MODEL), lambda i: (i, 0)),
            const((D_MODEL, Q_LORA)), const((D_MODEL, KV_LORA)),
            const((D_MODEL, LANES)), const((D_MODEL, LANES)),
            const((D_MODEL, 3 * RW_WIDTH)), const((D_MODEL, RW_LORA_PAD)),
            const((1, Q_LORA)), const((1, KV_LORA)),
            const((Q_LORA, H * LANES)), const((Q_LORA, H * LANES)),
            const((KV_LORA, H * LANES)), const((KV_LORA, H * LANES)),
            pl.BlockSpec((tm, LANES), lambda i: (i % nt_seq, 0)),
            pl.BlockSpec((tm, LANES), lambda i: (i % nt_seq, 0)),
        ],
        out_specs=[
            pl.BlockSpec((H, tm, LANES), lambda i: (0, i, 0)),
            pl.BlockSpec((H, tm, LANES), lambda i: (0, i, 0)),
            pl.BlockSpec((H, tm, LANES), lambda i: (0, i, 0)),
            pl.BlockSpec((tm, 3 * RW_WIDTH), lambda i: (i, 0)),
            pl.BlockSpec((tm, RW_LORA_PAD), lambda i: (i, 0)),
        ],
        out_shape=[
            jax.ShapeDtypeStruct((H, T, LANES), BF16),
            jax.ShapeDtypeStruct((H, T, LANES), BF16),
            jax.ShapeDtypeStruct((H, T, LANES), BF16),
            jax.ShapeDtypeStruct((T, 3 * RW_WIDTH), F32),
            jax.ShapeDtypeStruct((T, RW_LORA_PAD), F32),
        ],
        compiler_params=_cparams(("parallel",)),
        name="even_proj",
    )(x, bw(wq), bw(wkv), bw(wkra), bw(wkrb), bw(wrkv), bw(wlora),
      q_norm.reshape(1, -1), kv_norm.reshape(1, -1),
      bw(wuq), bw(wuqs), bw(wuk), bw(wuv), cos_t, sin_t)
    return outs


def _attn_kernel(q_ref, k_ref, v_ref, o_ref, *, n_kv, tk):
    qa = q_ref[0]
    qb = q_ref[1]
    tq = qa.shape[0]

    def one_head(q, kk, vv, m, acc):
        s = lax.dot_general(q, kk, _NT, preferred_element_type=F32)
        m_new = jnp.maximum(m, jnp.max(s, axis=-1, keepdims=True))
        alpha = jnp.exp2(m - m_new)
        p = jnp.exp2((s - m_new).astype(BF16))
        acc = alpha * acc + jnp.dot(p, vv, preferred_element_type=F32)
        return m_new, acc

    def body(c, carry):
        ma, acca, mb, accb = carry
        rows = pl.ds(pl.multiple_of(c * tk, tk), tk)
        ma, acca = one_head(qa, k_ref[0, rows, :], v_ref[0, rows, :], ma, acca)
        mb, accb = one_head(qb, k_ref[1, rows, :], v_ref[1, rows, :], mb, accb)
        return ma, acca, mb, accb

    m0 = jnp.full((tq, 1), -jnp.inf, F32)
    a0 = jnp.zeros((tq, LANES), F32)
    _, acca, _, accb = lax.fori_loop(0, n_kv, body, (m0, a0, m0, a0), unroll=2)
    lane = lax.broadcasted_iota(jnp.int32, (tq, LANES), 1)
    oa = acca / acca[:, MLA_V:MLA_V + 1]
    ob = pltpu.roll(accb / accb[:, MLA_V:MLA_V + 1], MLA_V, axis=1)
    o_ref[...] = jnp.where(lane < MLA_V, oa, ob).astype(o_ref.dtype)


def _attention(q, k, v, B, S, *, tq_pref=1024, tk_pref=512):
    H, T, _ = q.shape
    tq = _tile(S, tq_pref)
    tk = _tile(S, tk_pref)
    nq = S // tq
    return pl.pallas_call(
        functools.partial(_attn_kernel, n_kv=S // tk, tk=tk),
        grid=(B, H // 2, nq),
        in_specs=[
            pl.BlockSpec((2, tq, LANES), lambda b, p, i: (p, b * nq + i, 0)),
            pl.BlockSpec((2, S, LANES), lambda b, p, i: (p, b, 0)),
            pl.BlockSpec((2, S, LANES), lambda b, p, i: (p, b, 0)),
        ],
        out_specs=pl.BlockSpec((tq, LANES), lambda b, p, i: (b * nq + i, p)),
        out_shape=jax.ShapeDtypeStruct((T, H * MLA_V), BF16),
        compiler_params=_cparams(("parallel", "parallel", "arbitrary")),
        name="mla_attention",
    )(q, k, v)


def _shifted(main_ref, prev_ref, next_ref, ext_ref, first, last, reach):
    tm = main_ref.shape[0]
    ext_ref[0:HALO, :] = jnp.where(first, 0.0, prev_ref[...])
    ext_ref[HALO:HALO + tm, :] = main_ref[...]
    ext_ref[HALO + tm:2 * HALO + tm, :] = jnp.where(last, 0.0, next_ref[...])
    del reach

    def load(d):
        return ext_ref[HALO + d:HALO + d + tm, :]

    return load


def _head_sum(x, ones_ref):
    return _dot_exact_rhs(x, ones_ref[...])


def _rwkv_prep_kernel(rkv_ref, rkvp_ref, rkvn_ref, lo_ref, lop_ref, lon_ref,
                      mu_rkv_ref, mu_lo_ref, wl_ref, bias_ref, vec_ref, ones_ref,
                      r_ref, k_ref, v_ref, kk_ref, kka_ref, lw_ref, bonus_ref, g_ref,
                      ext_rkv, ext_lo, *, nt_seq):
    i = pl.program_id(0)
    first = (i % nt_seq) == 0
    last = (i % nt_seq) == nt_seq - 1
    W = RW_WIDTH

    ld = _shifted(rkv_ref, rkvp_ref, rkvn_ref, ext_rkv, first, last, 1)
    cur = rkv_ref[...]
    rkv = cur + (0.5 * (ld(-1) + ld(1)) - cur) * mu_rkv_ref[...]
    ld2 = _shifted(lo_ref, lop_ref, lon_ref, ext_lo, first, last, 1)
    cur2 = lo_ref[...]
    lo = cur2 + (0.5 * (ld2(-1) + ld2(1)) - cur2) * mu_lo_ref[...]

    r = rkv[:, 0:W]
    k = rkv[:, W:2 * W]
    v = rkv[:, 2 * W:3 * W]

    lane = lax.broadcasted_iota(jnp.int32, lo.shape, 1)
    act = jnp.where(lane < 2 * DECAY_LORA, jnp.tanh(lo),
                    jnp.where(lane < 2 * DECAY_LORA + AAA_LORA, lo, jax.nn.sigmoid(lo)))
    proj = jnp.dot(act.astype(BF16), wl_ref[...], preferred_element_type=F32) + bias_ref[...]
    u_f = proj[:, 0:W]
    u_b = proj[:, W:2 * W]
    a = jax.nn.sigmoid(proj[:, 2 * W:3 * W])
    g = proj[:, 3 * W:4 * W]

    k_k = vec_ref[0:1, :]
    k_a = vec_ref[1:2, :]
    r_k = vec_ref[2:3, :]

    c = math.exp(-0.5)
    lw_ref[0] = -c * jax.nn.sigmoid(u_f)
    lw_ref[1] = -c * jax.nn.sigmoid(u_b)

    kx = k * k_k
    kk = kx * lax.rsqrt(_head_sum(kx * kx, ones_ref) + 1e-12)
    k2 = k * (1.0 + (a - 1.0) * k_a)

    r_ref[...] = r
    k_ref[...] = k2
    v_ref[...] = v
    kk_ref[...] = kk
    kka_ref[...] = kk * a
    bonus_ref[...] = _head_sum(r * k2 * r_k, ones_ref) * v
    g_ref[...] = g


def _rwkv_prep(rkv, lora, S, mu, w0, w2, a0, a2, g2, k_k, k_a, r_k, ones_blk, *, tm_pref=256):
    T = rkv.shape[0]
    tm = _tile(S, tm_pref)
    nt_seq = S // tm
    W = RW_WIDTH
    hb = tm // HALO
    last_hb = T // HALO - 1

    mu_rkv = mu[:3 * W].reshape(1, -1)
    mu_lo = jnp.pad(mu[3 * W:], (0, RW_LORA_PAD - RW_LORA)).reshape(1, -1)
    wl = jnp.zeros((RW_LORA_PAD, 4 * W), F32)
    o1, o2, o3 = DECAY_LORA, 2 * DECAY_LORA, 2 * DECAY_LORA + AAA_LORA
    wl = wl.at[0:o1, 0:W].set(w2[0])
    wl = wl.at[o1:o2, W:2 * W].set(w2[1])
    wl = wl.at[o2:o3, 2 * W:3 * W].set(a2)
    wl = wl.at[o3:o3 + GATE_LORA, 3 * W:4 * W].set(g2)
    bias = jnp.concatenate([w0[0], w0[1], a0, jnp.zeros((W,), F32)]).reshape(1, -1)
    vecs = jnp.concatenate([k_k.reshape(1, W), k_a.reshape(1, W), r_k.reshape(1, W),
                            jnp.zeros((SUBLANES - 3, W), F32)], axis=0)

    def tiled(c):
        return pl.BlockSpec((tm, c), lambda i: (i, 0))

    def prev(c):
        return pl.BlockSpec((HALO, c), lambda i: (jnp.maximum(i * hb - 1, 0), 0))

    def nxt(c):
        return pl.BlockSpec((HALO, c), lambda i: (jnp.minimum((i + 1) * hb, last_hb), 0))

    def const(shape):
        return pl.BlockSpec(shape, lambda i: (0,) * len(shape))

    o512 = jax.ShapeDtypeStruct((T, W), F32)
    return pl.pallas_call(
        functools.partial(_rwkv_prep_kernel, nt_seq=nt_seq),
        grid=(T // tm,),
        in_specs=[tiled(3 * W), prev(3 * W), nxt(3 * W),
                  tiled(RW_LORA_PAD), prev(RW_LORA_PAD), nxt(RW_LORA_PAD),
                  const((1, 3 * W)), const((1, RW_LORA_PAD)),
                  const((RW_LORA_PAD, 4 * W)), const((1, 4 * W)),
                  const((SUBLANES, W)), const((W, W))],
        out_specs=[tiled(W), tiled(W), tiled(W), tiled(W), tiled(W),
                   pl.BlockSpec((2, tm, W), lambda i: (0, i, 0)),
                   tiled(W), tiled(W)],
        out_shape=[o512, o512, o512, o512, o512,
                   jax.ShapeDtypeStruct((2, T, W), F32), o512, o512],
        scratch_shapes=[pltpu.VMEM((tm + 2 * HALO, 3 * W), F32),
                        pltpu.VMEM((tm + 2 * HALO, RW_LORA_PAD), F32)],
        compiler_params=_cparams(("parallel",)),
        name="rwkv_prep",
    )(rkv, rkv, rkv, lora, lora, lora, mu_rkv, mu_lo, wl.astype(BF16), bias, vecs, ones_blk)


def _b(a):
    return a.astype(BF16)


def _unit_tri_inverse_all(Ms, blk_masks, eye):
    n = Ms[0].shape[0]
    Md = [jnp.where(blk_masks[RW_INV_BLOCK], M, 0.0) for M in Ms]
    Mdb = [_b(M) for M in Md]
    M2 = [_dg(a, a, _NN) for a in Mdb]
    M2b = [_b(a) for a in M2]
    M4 = [_dg(a, a, _NN) for a in M2b]
    M4b = [_b(a) for a in M4]
    M8 = [_dg(a, a, _NN) for a in M4b]
    T = [_dg(_b(eye - a), _b(eye + c), _NN) for a, c in zip(Md, M2)]
    T = [_dg(_b(t), _b(eye + c), _NN) for t, c in zip(T, M4)]
    T = [_dg(_b(t), _b(eye + c), _NN) for t, c in zip(T, M8)]
    s = RW_INV_BLOCK
    while s < n:
        off = jnp.logical_and(blk_masks[2 * s], jnp.logical_not(blk_masks[s]))
        Tb = [_b(t) for t in T]
        TC = [_dg(tb, _b(jnp.where(off, M, 0.0)), _NN) for tb, M in zip(Tb, Ms)]
        T = [t - _dg(_b(tc), tb, _NN) for t, tc, tb in zip(T, TC, Tb)]
        s *= 2
    return T


def _rwkv_scan_kernel(r_ref, k_ref, v_ref, kk_ref, kka_ref, lw_ref, y_ref, s_ref, *, n_sub):
    d = pl.program_id(0)
    c = pl.program_id(2)
    L = RW_CHUNK
    N = RW_HEAD
    H = RW_HEADS
    fwd = d == 0

    @pl.when(c == 0)
    def _():
        s_ref[...] = jnp.zeros_like(s_ref)

    row = lax.broadcasted_iota(jnp.int32, (L, L), 0)
    col = lax.broadcasted_iota(jnp.int32, (L, L), 1)
    ahead = (row - col) * (1 - 2 * d)
    earlier = ahead > 0
    upto = ahead >= 0
    cum_mask = jnp.where(upto, 1.0, 0.0).astype(BF16)
    eye = jnp.where(row == col, 1.0, 0.0).astype(F32)
    blk_masks = {}
    s = RW_INV_BLOCK
    while s <= L:
        blk_masks[s] = (row // s) == (col // s)
        s *= 2

    rows_of, w_tot = [], []
    kap, Rh, Ah, Kh, Vs = [], [], [], [], []
    for sidx in range(n_sub):
        j = jnp.where(fwd, sidx, n_sub - 1 - sidx)
        rows = pl.ds(pl.multiple_of(j * L, L), L)
        rows_of.append(rows)
        lw = lw_ref[0, rows, :]
        cs = _dot_exact_lhs(cum_mask, lw)
        e_neg = jnp.exp(-cs)
        ah = kka_ref[rows, :] * e_neg
        kh = k_ref[rows, :] * e_neg
        kp = kk_ref[rows, :] * jnp.exp(cs - lw)
        rh = r_ref[rows, :] * jnp.exp(cs)
        vv = v_ref[rows, :]
        w_tot.append(jnp.exp(jnp.where(fwd, cs[L - 1:L, :], cs[0:1, :])))
        for h in range(H):
            sl = slice(h * N, (h + 1) * N)
            kap.append(kp[:, sl])
            Rh.append(rh[:, sl])
            Ah.append(ah[:, sl])
            Kh.append(kh[:, sl])
            Vs.append(vv[:, sl])

    Vb = [_b(v) for v in Vs]
    Zb = [_b(jnp.concatenate([a, k], axis=0)) for a, k in zip(Ah, Kh)]
    G = [_dg(_b(jnp.concatenate([kp, rh], axis=0)), z, _NT) for kp, rh, z in zip(kap, Rh, Zb)]
    Mak = [jnp.where(earlier, g[0:L, 0:L], 0.0) for g in G]
    Mkk = [_b(jnp.where(earlier, g[0:L, L:2 * L], 0.0)) for g in G]
    Mar = [_b(jnp.where(upto, g[L:2 * L, 0:L], 0.0)) for g in G]
    Mkr = [_b(jnp.where(upto, g[L:2 * L, L:2 * L], 0.0)) for g in G]
    Tinv = _unit_tri_inverse_all(Mak, blk_masks, eye)
    MV = [_dg(m, v, _NN) for m, v in zip(Mkk, Vb)]
    A12 = [_dg(_b(t), _b(jnp.concatenate([kp, mv], axis=1)), _NN) for t, kp, mv in zip(Tinv, kap, MV)]
    MA = [_dg(m, _b(a), _NN) for m, a in zip(Mar, A12)]
    Q1 = [_b(rh - ma[:, 0:N]) for rh, ma in zip(Rh, MA)]
    Y0 = [_dg(m, v, _NN) - ma[:, N:2 * N] for m, v, ma in zip(Mkr, Vb, MA)]
    Pm = [_b(eye - _dg(_b(a[:, 0:N].T), _b(ah), _NN)) for a, ah in zip(A12, Ah)]
    Q2 = [_dg(_b(jnp.concatenate([v, -a[:, N:2 * N]], axis=0).T),
              _b(jnp.concatenate([kh, ah], axis=0)), _NN)
          for v, a, kh, ah in zip(Vs, A12, Kh, Ah)]

    S = [s_ref[h] for h in range(H)]
    for sidx in range(n_sub):
        ys = []
        for h in range(H):
            p = sidx * H + h
            s_hi, s_lo = _split2(S[h])
            ys.append(_dg(Q1[p], s_hi, _NT) + _dg(Q1[p], s_lo, _NT) + Y0[p])
            S[h] = ((_dg(s_hi, Pm[p], _NN) + _dg(s_lo, Pm[p], _NN) + Q2[p])
                    * w_tot[sidx][:, h * N:(h + 1) * N])
        y_ref[0, rows_of[sidx], :] = jnp.concatenate(ys, axis=1)
    for h in range(H):
        s_ref[h] = S[h]


def _rwkv_scan(r, k, v, kk, kka, lw, B, S, *, tb_pref=256):
    T, W = r.shape
    tb = _tile(S, tb_pref)
    nb = S // tb

    def tmap(d, b, c):
        return (b * nb + jnp.where(d == 0, c, nb - 1 - c), 0)

    tiled = pl.BlockSpec((tb, W), tmap)
    dir_tiled = pl.BlockSpec((1, tb, W), lambda d, b, c: (d,) + tmap(d, b, c))
    return pl.pallas_call(
        functools.partial(_rwkv_scan_kernel, n_sub=tb // RW_CHUNK),
        grid=(2, B, nb),
        in_specs=[tiled, tiled, tiled, tiled, tiled, dir_tiled],
        out_specs=dir_tiled,
        out_shape=jax.ShapeDtypeStruct((2, T, W), F32),
        scratch_shapes=[pltpu.VMEM((RW_HEADS, RW_HEAD, RW_HEAD), F32)],
        compiler_params=_cparams(("parallel", "parallel", "arbitrary")),
        name="rwkv_scan",
    )(r, k, v, kk, kka, lw)


def _even_out_kernel(x_ref, oa_ref, y_ref, bonus_ref, g_ref, gn_ref, ones_ref,
                     wa_ref, wb_ref, lg_ref, lb_ref, o_ref):
    y = y_ref[0] + y_ref[1]
    inv_n = 1.0 / RW_HEAD
    mean = _head_sum(y, ones_ref) * inv_n
    dy = y - mean
    var = _head_sum(dy * dy, ones_ref) * inv_n
    yn = dy * lax.rsqrt(var + RW_GN_EPS) * gn_ref[0:1, :] + gn_ref[1:2, :]
    ob = ((yn + bonus_ref[...]) * g_ref[...]).astype(BF16)
    m = (jnp.dot(oa_ref[...], wa_ref[...], preferred_element_type=F32)
         + jnp.dot(ob, wb_ref[...], preferred_element_type=F32))
    o_ref[...] = _ln(DN_ALPHA * x_ref[...] + m, lg_ref[...], lb_ref[...])


def _even_out(x, oa, y, bonus, g, gn_g, gn_b, ones_blk, w_out, ln_g, ln_b, *, tm_pref=256):
    T = x.shape[0]
    tm = _tile(T, tm_pref)
    W = RW_WIDTH
    gn = jnp.concatenate([gn_g.reshape(1, W), gn_b.reshape(1, W), jnp.zeros((SUBLANES - 2, W), F32)], axis=0)
    wa = w_out[:MLA_HEADS * MLA_V].astype(BF16)
    wb = w_out[MLA_HEADS * MLA_V:].astype(BF16)

    def tiled(c):
        return pl.BlockSpec((tm, c), lambda i: (i, 0))

    def const(shape):
        return pl.BlockSpec(shape, lambda i: (0,) * len(shape))

    return pl.pallas_call(
        _even_out_kernel,
        grid=(T // tm,),
        in_specs=[tiled(D_MODEL), tiled(MLA_HEADS * MLA_V),
                  pl.BlockSpec((2, tm, W), lambda i: (0, i, 0)),
                  tiled(W), tiled(W), const((SUBLANES, W)), const((W, W)),
                  const((MLA_HEADS * MLA_V, D_MODEL)), const((W, D_MODEL)),
                  const((1, D_MODEL)), const((1, D_MODEL))],
        out_specs=tiled(D_MODEL),
        out_shape=jax.ShapeDtypeStruct((T, D_MODEL), F32),
        compiler_params=_cparams(("parallel",)),
        name="even_out",
    )(x, oa, y, bonus, g, gn, ones_blk, wa, wb, ln_g.reshape(1, -1), ln_b.reshape(1, -1))


def _matmul_kernel(x_ref, w_ref, o_ref):
    o_ref[...] = jnp.dot(x_ref[...].astype(BF16), w_ref[...], preferred_element_type=F32).astype(o_ref.dtype)


def _matmul(x, w, *, tm_pref=512, tn_pref=1024, out_dtype=F32, name="matmul"):
    T, K = x.shape
    N = w.shape[1]
    tm = _tile(T, tm_pref)
    tn = N if N <= tn_pref else tn_pref
    assert N % tn == 0
    return pl.pallas_call(
        _matmul_kernel,
        grid=(T // tm, N // tn),
        in_specs=[pl.BlockSpec((tm, K), lambda i, j: (i, 0)),
                  pl.BlockSpec((K, tn), lambda i, j: (0, j))],
        out_specs=pl.BlockSpec((tm, tn), lambda i, j: (i, j)),
        out_shape=jax.ShapeDtypeStruct((T, N), out_dtype),
        compiler_params=_cparams(("parallel", "arbitrary")),
        name=name,
    )(x, w.astype(BF16))


def _conv_kernel(u_ref, up_ref, un_ref, w_ref, b_ref, xs_ref, bm_ref, cm_ref, ext_ref, *, nt_seq):
    i = pl.program_id(0)
    first = (i % nt_seq) == 0
    last = (i % nt_seq) == nt_seq - 1
    ld = _shifted(u_ref, up_ref, un_ref, ext_ref, first, last, M_CONV // 2)
    acc = b_ref[...] + ld(-2) * w_ref[0:1, :]
    for j in range(1, M_CONV):
        acc = acc + ld(j - M_CONV // 2) * w_ref[j:j + 1, :]
    y = acc * jax.nn.sigmoid(acc)
    xs_ref[...] = y[:, 0:M_INNER]
    bm_ref[...] = y[:, M_INNER:M_INNER + M_GROUPS * M_STATE]
    cm_ref[...] = y[:, M_INNER + M_GROUPS * M_STATE:]


def _conv_silu(xbc, S, conv_w, conv_b, *, tm_pref=256):
    T, C = xbc.shape
    tm = _tile(S, tm_pref)
    nt_seq = S // tm
    hb = tm // HALO
    last_hb = T // HALO - 1
    wpad = jnp.concatenate([conv_w, jnp.zeros((SUBLANES - M_CONV, C), F32)], axis=0)
    GN = M_GROUPS * M_STATE
    return pl.pallas_call(
        functools.partial(_conv_kernel, nt_seq=nt_seq),
        grid=(T // tm,),
        in_specs=[pl.BlockSpec((tm, C), lambda i: (i, 0)),
                  pl.BlockSpec((HALO, C), lambda i: (jnp.maximum(i * hb - 1, 0), 0)),
                  pl.BlockSpec((HALO, C), lambda i: (jnp.minimum((i + 1) * hb, last_hb), 0)),
                  pl.BlockSpec((SUBLANES, C), lambda i: (0, 0)),
                  pl.BlockSpec((1, C), lambda i: (0, 0))],
        out_specs=[pl.BlockSpec((tm, M_INNER), lambda i: (i, 0)),
                   pl.BlockSpec((tm, GN), lambda i: (i, 0)),
                   pl.BlockSpec((tm, GN), lambda i: (i, 0))],
        out_shape=[jax.ShapeDtypeStruct((T, M_INNER), F32),
                   jax.ShapeDtypeStruct((T, GN), F32),
                   jax.ShapeDtypeStruct((T, GN), F32)],
        scratch_shapes=[pltpu.VMEM((tm + 2 * HALO, C), F32)],
        compiler_params=_cparams(("parallel",)),
        name="conv_silu",
    )(xbc, xbc, xbc, wpad, conv_b.reshape(1, -1))


def _ssd_kernel(xs_ref, bm_ref, cm_ref, dt_ref, dtb_ref, alog_ref, exp_ref, y_ref, st_ref):
    d = pl.program_id(0)
    c = pl.program_id(2)
    L = SSD_CHUNK
    P = M_HEADDIM
    GW = M_GROUP_W
    fwd = d == 0

    @pl.when(c == 0)
    def _():
        st_ref[...] = jnp.zeros_like(st_ref)

    row = lax.broadcasted_iota(jnp.int32, (L, L), 0)
    col = lax.broadcasted_iota(jnp.int32, (L, L), 1)
    upto = (row - col) * (1 - 2 * d) >= 0
    cum_mask = jnp.where(upto, 1.0, 0.0).astype(BF16)

    z = dt_ref[...] + dtb_ref[...]
    dt = jnp.maximum(z, 0.0) + jnp.log1p(jnp.exp(-jnp.abs(z)))
    a = dt * (-jnp.exp(alog_ref[...]))
    cs = _dot_exact_lhs(cum_mask, a)
    tot = jnp.where(fwd, cs[L - 1:L, :], cs[0:1, :])
    wgt = dt * jnp.exp(tot - cs)

    expand = exp_ref[0]
    cs_x = _dot2_rhs(cs, expand)
    wgt_x = _dot2_rhs(wgt, expand)
    tot_x = jnp.where(fwd, cs_x[L - 1:L, :], cs_x[0:1, :])

    xs = xs_ref[...]
    xs_b = xs.astype(BF16)
    into_state = (xs * wgt_x).astype(BF16)
    from_state = jnp.exp(cs_x)

    cs_d = jnp.where(fwd, cs, pltpu.roll(cs, LANES - M_HEADS, axis=1))
    dt_d = jnp.where(fwd, dt, pltpu.roll(dt, LANES - M_HEADS, axis=1))
    csT = cs_d.T
    dtT = dt_d.T
    lane = lax.broadcasted_iota(jnp.int32, (L, LANES), 1)
    lo_half = lane < P

    ys = []
    for g in range(M_GROUPS):
        gs = slice(g * GW, (g + 1) * GW)
        ns = slice(g * M_STATE, (g + 1) * M_STATE)
        Bg = bm_ref[:, ns]
        Cg = cm_ref[:, ns].astype(BF16)
        CB = lax.dot_general(Cg, Bg.astype(BF16), _NT, preferred_element_type=F32)
        st = st_ref[g]
        y_off = jnp.dot(Cg, st.astype(BF16), preferred_element_type=F32) * from_state[:, gs]
        y_pairs = []
        for pp in range(M_HPG // 2):
            ms = []
            for hh in (2 * pp, 2 * pp + 1):
                hidx = g * M_HPG + hh
                col_l = cs_d[:, hidx:hidx + 1]
                row_s = csT[hidx:hidx + 1, :]
                dec = jnp.exp(jnp.where(upto, col_l - row_s, -jnp.inf)) * dtT[hidx:hidx + 1, :]
                ms.append((CB * dec).astype(BF16))
            xpair = xs_b[:, g * GW + pp * LANES:g * GW + (pp + 1) * LANES]
            zero = jnp.zeros_like(xpair)
            rhs = jnp.concatenate([jnp.where(lo_half, xpair, zero), jnp.where(lo_half, zero, xpair)], axis=0)
            y_pairs.append(jnp.dot(jnp.concatenate(ms, axis=1), rhs, preferred_element_type=F32))
        ys.append(jnp.concatenate(y_pairs, axis=1) + y_off)
        st_ref[g] = st * jnp.exp(tot_x[:, gs]) + jnp.dot(Bg.T.astype(BF16), into_state[:, gs],
                                                          preferred_element_type=F32)
    y_ref[0] = jnp.concatenate(ys, axis=1).astype(y_ref.dtype)


def _ssd(xs, bm, cm, dt_raw, dt_bias, a_log, B, S):
    T = xs.shape[0]
    L = SSD_CHUNK
    nc = S // L
    GN = M_GROUPS * M_STATE

    def pad_lanes(v):
        return jnp.pad(v.reshape(1, -1), ((0, 0), (0, LANES - v.size)))

    heads = jnp.arange(M_INNER) // M_HEADDIM
    expand = jnp.stack([
        (jnp.arange(LANES)[:, None] == heads[None, :]),
        (jnp.arange(LANES)[:, None] == (heads[None, :] + M_HEADS)),
    ]).astype(BF16)

    def tmap(d, b, c):
        return (b * nc + jnp.where(d == 0, c, nc - 1 - c), 0)

    def tiled(w):
        return pl.BlockSpec((L, w), tmap)

    return pl.pallas_call(
        _ssd_kernel,
        grid=(2, B, nc),
        in_specs=[tiled(M_INNER), tiled(GN), tiled(GN), tiled(LANES),
                  pl.BlockSpec((1, LANES), lambda d, b, c: (0, 0)),
                  pl.BlockSpec((1, LANES), lambda d, b, c: (0, 0)),
                  pl.BlockSpec((1, LANES, M_INNER), lambda d, b, c: (d, 0, 0))],
        out_specs=pl.BlockSpec((1, L, M_INNER), lambda d, b, c: (d,) + tmap(d, b, c)),
        out_shape=jax.ShapeDtypeStruct((2, T, M_INNER), BF16),
        scratch_shapes=[pltpu.VMEM((M_GROUPS, M_STATE, M_GROUP_W), F32)],
        compiler_params=_cparams(("parallel", "parallel", "arbitrary")),
        name="ssd_scan",
    )(xs, bm, cm, dt_raw, pad_lanes(dt_bias), pad_lanes(a_log), expand)


def _odd_out_kernel(x_ref, y_ref, xs_ref, z_ref, dvec_ref, ng_ref, w_ref, lg_ref, lb_ref, o_ref):
    z = z_ref[...].astype(F32)
    y = ((y_ref[0].astype(F32) + y_ref[1].astype(F32) + xs_ref[...] * dvec_ref[...])
         * (z * jax.nn.sigmoid(z)))
    parts = []
    for g in range(M_GROUPS):
        yg = y[:, g * M_GROUP_W:(g + 1) * M_GROUP_W]
        parts.append(yg * lax.rsqrt(jnp.mean(yg * yg, axis=-1, keepdims=True) + RMS_EPS))
    yn = (jnp.concatenate(parts, axis=1) * ng_ref[...]).astype(BF16)
    m = jnp.dot(yn, w_ref[...], preferred_element_type=F32)
    o_ref[...] = _ln(DN_ALPHA * x_ref[...] + m, lg_ref[...], lb_ref[...])


def _odd_out(x, y, xs, z, d_skip, norm_g, w_out, ln_g, ln_b, *, tm_pref=256):
    T = x.shape[0]
    tm = _tile(T, tm_pref)
    dvec = jnp.repeat(d_skip, M_HEADDIM).reshape(1, -1)

    def tiled(c):
        return pl.BlockSpec((tm, c), lambda i: (i, 0))

    def const(shape):
        return pl.BlockSpec(shape, lambda i: (0,) * len(shape))

    return pl.pallas_call(
        _odd_out_kernel,
        grid=(T // tm,),
        in_specs=[tiled(D_MODEL), pl.BlockSpec((2, tm, M_INNER), lambda i: (0, i, 0)),
                  tiled(M_INNER), tiled(M_INNER), const((1, M_INNER)), const((1, M_INNER)),
                  const((M_INNER, D_MODEL)), const((1, D_MODEL)), const((1, D_MODEL))],
        out_specs=tiled(D_MODEL),
        out_shape=jax.ShapeDtypeStruct((T, D_MODEL), F32),
        compiler_params=_cparams(("parallel",)),
        name="odd_out",
    )(x, y, xs, z, dvec, norm_g.reshape(1, -1), w_out.astype(BF16),
      ln_g.reshape(1, -1), ln_b.reshape(1, -1))


def _even_mixer_ln(x, B, S, j, P, ln_g, ln_b, ones_blk):
    q, k, v, rkv, lora = _even_proj(x, S, P['w_in_even'][j], P['mla_q_norm'][j], P['mla_w_uq'][j],
                                    P['mla_kv_norm'][j], P['mla_w_ukv'][j])
    oa = _attention(q, k, v, B, S)
    r, k2, vv, kk, kka, lw, bonus, g = _rwkv_prep(
        rkv, lora, S, P['rw_mu'][j], P['rw_w0'][j], P['rw_w2'][j], P['rw_a0'][j], P['rw_a2'][j],
        P['rw_g2'][j], P['rw_k_k'][j], P['rw_k_a'][j], P['rw_r_k'][j], ones_blk)
    y = _rwkv_scan(r, k2, vv, kk, kka, lw, B, S)
    return _even_out(x, oa, y, bonus, g, P['rw_gn_g'][j], P['rw_gn_b'][j], ones_blk,
                     P['w_out_even'][j], ln_g, ln_b)


def _odd_mixer_ln(x, B, S, j, P, ln_g, ln_b):
    w_in = P['w_in_odd'][j]
    z = _matmul(x, w_in[:, :M_INNER], out_dtype=BF16, name="odd_proj_z")
    xbc = _matmul(x, w_in[:, M_INNER:M_INNER + M_CONV_DIM], name="odd_proj_xbc")
    w_dt = jnp.pad(w_in[:, M_INNER + M_CONV_DIM:], ((0, 0), (0, LANES - 2 * M_HEADS)))
    dt_raw = _matmul(x, w_dt, name="odd_proj_dt")
    xs, bm, cm = _conv_silu(xbc, S, P['m_conv_w'][j], P['m_conv_b'][j])
    y = _ssd(xs, bm, cm, dt_raw, P['m_dt_bias'][j].reshape(-1), P['m_A_log'][j].reshape(-1), B, S)
    return _odd_out(x, y, xs, z, P['m_D'][j], P['m_norm_g'][j], P['w_out_odd'][j], ln_g, ln_b)


def _trunk(x3, P):
    B, S, D = x3.shape
    x = x3.reshape(B * S, D)
    head_id = jnp.arange(RW_WIDTH) // RW_HEAD
    ones_blk = (head_id[:, None] == head_id[None, :]).astype(BF16)
    for i in range(DEPTH):
        x = _ffn_ln(x, P['ffn1_in'][i], P['ffn1_out'][i], P['ln_g'][i, 0], P['ln_b'][i, 0])
        if i % 2 == 0:
            x = _even_mixer_ln(x, B, S, i // 2, P, P['ln_g'][i, 1], P['ln_b'][i, 1], ones_blk)
        else:
            x = _odd_mixer_ln(x, B, S, i // 2, P, P['ln_g'][i, 1], P['ln_b'][i, 1])
        x = _ffn_ln(x, P['ffn2_in'][i], P['ffn2_out'][i], P['ln_g'][i, 2], P['ln_b'][i, 2])
    return x.reshape(B, S, D)


def kernel(x_prompt, x_sample, ffn1_in, ffn1_out, ffn2_in, ffn2_out, ln_g, ln_b, w_in_even, w_out_even, mla_q_norm, mla_w_uq, mla_kv_norm, mla_w_ukv, rw_mu, rw_w0, rw_w2, rw_a0, rw_a2, rw_g2, rw_k_k, rw_k_a, rw_r_k, rw_gn_g, rw_gn_b, w_in_odd, m_conv_w, m_conv_b, m_dt_bias, m_A_log, m_D, m_norm_g, w_out_odd):
    P = dict(ffn1_in=ffn1_in, ffn1_out=ffn1_out, ffn2_in=ffn2_in, ffn2_out=ffn2_out,
             ln_g=ln_g, ln_b=ln_b, w_in_even=w_in_even, w_out_even=w_out_even,
             mla_q_norm=mla_q_norm, mla_w_uq=mla_w_uq, mla_kv_norm=mla_kv_norm,
             mla_w_ukv=mla_w_ukv, rw_mu=rw_mu, rw_w0=rw_w0, rw_w2=rw_w2, rw_a0=rw_a0,
             rw_a2=rw_a2, rw_g2=rw_g2, rw_k_k=rw_k_k, rw_k_a=rw_k_a, rw_r_k=rw_r_k,
             rw_gn_g=rw_gn_g, rw_gn_b=rw_gn_b, w_in_odd=w_in_odd, m_conv_w=m_conv_w,
             m_conv_b=m_conv_b, m_dt_bias=m_dt_bias, m_A_log=m_A_log, m_D=m_D,
             m_norm_g=m_norm_g, w_out_odd=w_out_odd)
    return (_trunk(x_prompt, P), _trunk(x_sample, P))
```

```python
import functools
import math

import jax
import jax.numpy as jnp
from jax import lax
from jax.experimental import pallas as pl
from jax.experimental.pallas import tpu as pltpu

F32 = jnp.float32
BF16 = jnp.bfloat16

D_MODEL = 1024
DEPTH = 4
DN_ALPHA = (2.0 * DEPTH) ** 0.25
LN_EPS = 1e-5
RMS_EPS = 1e-6
D_FF = 2816

MLA_HEADS = 8
MLA_NOPE = 64
MLA_ROPE = 32
MLA_QK = MLA_NOPE + MLA_ROPE
MLA_V = 64
Q_LORA = 384
KV_LORA = 256
ROPE_THETA = 10000.0
MLA_SCALE = MLA_QK ** -0.5
MLA_IN = Q_LORA + KV_LORA + MLA_ROPE

RW_HEADS = 8
RW_HEAD = 64
RW_WIDTH = RW_HEADS * RW_HEAD
DECAY_LORA = 64
AAA_LORA = 64
GATE_LORA = 128
RW_GN_EPS = 64e-5
RW_LORA = 2 * DECAY_LORA + AAA_LORA + GATE_LORA
RW_LORA_PAD = 384

M_INNER = 2 * D_MODEL
M_HEADDIM = 64
M_HEADS = M_INNER // M_HEADDIM
M_GROUPS = 4
M_HPG = M_HEADS // M_GROUPS
M_STATE = 128
M_CONV = 5
SSD_CHUNK = 128
M_CONV_DIM = M_INNER + 2 * M_GROUPS * M_STATE
M_GROUP_W = M_INNER // M_GROUPS

LANES = 128
SUBLANES = 8
HALO = SUBLANES
VMEM_LIMIT = 56 * 1024 * 1024

RW_CHUNK = 64
RW_INV_BLOCK = 16


def _tile(n, pref):
    t = min(n, pref)
    while n % t:
        t -= SUBLANES
    return t


def _cparams(sem):
    return pltpu.CompilerParams(dimension_semantics=sem, vmem_limit_bytes=VMEM_LIMIT)


def _ln(y, g, b):
    mu = jnp.mean(y, axis=-1, keepdims=True)
    d = y - mu
    var = jnp.mean(d * d, axis=-1, keepdims=True)
    return d * lax.rsqrt(var + LN_EPS) * g + b


def _bdot(a, b):
    return jnp.dot(a.astype(BF16), b.astype(BF16), preferred_element_type=F32)


def _split3(a):
    hi = a.astype(BF16)
    r1 = a - hi.astype(F32)
    mid = r1.astype(BF16)
    lo = (r1 - mid.astype(F32)).astype(BF16)
    return hi, mid, lo


def _split2(a):
    hi = a.astype(BF16)
    lo = (a - hi.astype(F32)).astype(BF16)
    return hi, lo


_NN = (((1,), (0,)), ((), ()))
_NT = (((1,), (1,)), ((), ()))


def _dg(a, b, dims):
    return lax.dot_general(a, b, dims, preferred_element_type=F32)


def _dot3(a, b, dims=_NN):
    ah, al = _split2(a)
    bh, bl = _split2(b)
    return _dg(ah, bh, dims) + (_dg(ah, bl, dims) + _dg(al, bh, dims))


def _dot_exact_lhs(a_bf16, b):
    b0, b1, b2 = _split3(b)
    return _dg(a_bf16, b0, _NN) + (_dg(a_bf16, b1, _NN) + _dg(a_bf16, b2, _NN))


def _dot_exact_rhs(a, b_bf16):
    a0, a1, a2 = _split3(a)
    return _dg(a0, b_bf16, _NN) + (_dg(a1, b_bf16, _NN) + _dg(a2, b_bf16, _NN))


def _dot2_rhs(a, b_bf16):
    a0, a1 = _split2(a)
    return _dg(a0, b_bf16, _NN) + _dg(a1, b_bf16, _NN)


def _ffn_kernel(x_ref, wg_ref, wu_ref, wo_ref, g_ref, b_ref, o_ref):
    x = x_ref[...]
    xb = x.astype(BF16)
    gate = jnp.dot(xb, wg_ref[...], preferred_element_type=F32)
    up = jnp.dot(xb, wu_ref[...], preferred_element_type=F32)
    h = (gate * jax.nn.sigmoid(gate) * up).astype(BF16)
    m = jnp.dot(h, wo_ref[...], preferred_element_type=F32)
    o_ref[...] = _ln(DN_ALPHA * x + 0.5 * m, g_ref[...], b_ref[...])


def _resident(shape, index_map):
    return pl.BlockSpec(shape, index_map, pipeline_mode=pl.Buffered(1))


def _ffn_ln(x, w_in, w_out, g, b, *, tm_pref=512):
    T = x.shape[0]
    tm = _tile(T, tm_pref)
    w_in = w_in.astype(BF16)
    w_out = w_out.astype(BF16)
    return pl.pallas_call(
        _ffn_kernel,
        grid=(T // tm,),
        in_specs=[
            pl.BlockSpec((tm, D_MODEL), lambda i: (i, 0)),
            _resident((D_MODEL, D_FF), lambda i: (0, 0)),
            _resident((D_MODEL, D_FF), lambda i: (0, 1)),
            _resident((D_FF, D_MODEL), lambda i: (0, 0)),
            _resident((1, D_MODEL), lambda i: (0, 0)),
            _resident((1, D_MODEL), lambda i: (0, 0)),
        ],
        out_specs=pl.BlockSpec((tm, D_MODEL), lambda i: (i, 0)),
        out_shape=jax.ShapeDtypeStruct((T, D_MODEL), F32),
        compiler_params=_cparams(("parallel",)),
        name="ffn_ln",
    )(x, w_in, w_in, w_out, g.reshape(1, -1), b.reshape(1, -1))


def _even_proj_kernel(x_ref, wq_ref, wkv_ref, wkra_ref, wkrb_ref, wrkv_ref, wlora_ref,
                      qg_ref, kvg_ref, wuq_ref, wuqs_ref, wuk_ref, wuv_ref,
                      cos_ref, sin_ref,
                      q_ref, k_ref, v_ref, rkv_ref, lora_ref):
    xb = x_ref[...].astype(BF16)
    rkv_ref[...] = jnp.dot(xb, wrkv_ref[...], preferred_element_type=F32)
    lora_ref[...] = jnp.dot(xb, wlora_ref[...], preferred_element_type=F32)

    cq = jnp.dot(xb, wq_ref[...], preferred_element_type=F32)
    ckv = jnp.dot(xb, wkv_ref[...], preferred_element_type=F32)
    cqn = (cq * lax.rsqrt(jnp.mean(cq * cq, axis=-1, keepdims=True) + RMS_EPS) * qg_ref[...]).astype(BF16)
    ckvn = (ckv * lax.rsqrt(jnp.mean(ckv * ckv, axis=-1, keepdims=True) + RMS_EPS) * kvg_ref[...]).astype(BF16)

    cos = cos_ref[...]
    sin = sin_ref[...]
    lane = lax.broadcasted_iota(jnp.int32, cos.shape, 1)
    cos_q = jnp.where(lane < MLA_NOPE, 1.0, cos)

    kr = (jnp.dot(xb, wkra_ref[...], preferred_element_type=F32) * cos
          + jnp.dot(xb, wkrb_ref[...], preferred_element_type=F32) * sin)

    qa = jnp.dot(cqn, wuq_ref[...], preferred_element_type=F32)
    qs = jnp.dot(cqn, wuqs_ref[...], preferred_element_type=F32)
    kn = jnp.dot(ckvn, wuk_ref[...], preferred_element_type=F32)
    vv = jnp.dot(ckvn, wuv_ref[...], preferred_element_type=F32)
    q_scale = MLA_SCALE * math.log2(math.e)
    one_col = jnp.where(lane == MLA_V, 1.0, 0.0)
    for h in range(MLA_HEADS):
        sl = slice(h * LANES, (h + 1) * LANES)
        q_ref[h] = ((qa[:, sl] * cos_q + qs[:, sl] * sin) * q_scale).astype(BF16)
        k_ref[h] = (kn[:, sl] + kr).astype(BF16)
        v_ref[h] = (vv[:, sl] + one_col).astype(BF16)


def _rope_swap(w):
    half = MLA_ROPE // 2
    return jnp.concatenate([-w[..., half:], w[..., :half]], axis=-1)


def _even_proj(x, S, w_in, q_norm, w_uq, kv_norm, w_ukv, *, tm_pref=256):
    T = x.shape[0]
    tm = _tile(S, tm_pref)
    nt_seq = S // tm
    H = MLA_HEADS

    wq = w_in[:, :Q_LORA]
    wkv = w_in[:, Q_LORA:Q_LORA + KV_LORA]
    wkr = w_in[:, Q_LORA + KV_LORA:MLA_IN]
    zk = jnp.zeros((D_MODEL, MLA_NOPE), F32)
    zt = jnp.zeros((D_MODEL, LANES - MLA_QK), F32)
    wkra = jnp.concatenate([zk, wkr, zt], axis=1)
    wkrb = jnp.concatenate([zk, _rope_swap(wkr), zt], axis=1)
    wrw = w_in[:, MLA_IN:]
    wrkv = wrw[:, :3 * RW_WIDTH]
    wlora = jnp.pad(wrw[:, 3 * RW_WIDTH:], ((0, 0), (0, RW_LORA_PAD - RW_LORA)))

    uq = w_uq.reshape(Q_LORA, H, MLA_QK)
    pad_q = jnp.zeros((Q_LORA, H, LANES - MLA_QK), F32)
    wuq = jnp.concatenate([uq, pad_q], axis=-1).reshape(Q_LORA, H * LANES)
    wuqs = jnp.concatenate([jnp.zeros((Q_LORA, H, MLA_NOPE), F32), _rope_swap(uq[..., MLA_NOPE:]), pad_q],
                           axis=-1).reshape(Q_LORA, H * LANES)
    ukv = w_ukv.reshape(KV_LORA, H, MLA_NOPE + MLA_V)
    wuk = jnp.concatenate([ukv[..., :MLA_NOPE], jnp.zeros((KV_LORA, H, LANES - MLA_NOPE), F32)],
                          axis=-1).reshape(KV_LORA, H * LANES)
    wuv = jnp.concatenate([ukv[..., MLA_NOPE:], jnp.zeros((KV_LORA, H, LANES - MLA_V), F32)],
                          axis=-1).reshape(KV_LORA, H * LANES)

    half = MLA_ROPE // 2
    inv = ROPE_THETA ** (-jnp.arange(half, dtype=F32) / half)
    ang = jnp.arange(S, dtype=F32)[:, None] * inv[None, :]
    zl = jnp.zeros((S, MLA_NOPE), F32)
    zr = jnp.zeros((S, LANES - MLA_QK), F32)
    cos_t = jnp.concatenate([zl, jnp.cos(ang), jnp.cos(ang), zr], axis=1)
    sin_t = jnp.concatenate([zl, jnp.sin(ang), jnp.sin(ang), zr], axis=1)

    def const(shape):
        return pl.BlockSpec(shape, lambda i: (0,) * len(shape))

    bw = lambda a: a.astype(BF16)
    outs = pl.pallas_call(
        _even_proj_kernel,
        grid=(T // tm,),
        in_specs=[
            pl.BlockSpec((tm, D_MODEL), lambda i: (i, 0)),
            const((D_MODEL, Q_LORA)), const((D_MODEL, KV_LORA)),
            const((D_MODEL, LANES)), const((D_MODEL, LANES)),
            const((D_MODEL, 3 * RW_WIDTH)), const((D_MODEL, RW_LORA_PAD)),
            const((1, Q_LORA)), const((1, KV_LORA)),
            const((Q_LORA, H * LANES)), const((Q_LORA, H * LANES)),
            const((KV_LORA, H * LANES)), const((KV_LORA, H * LANES)),
            pl.BlockSpec((tm, LANES), lambda i: (i % nt_seq, 0)),
            pl.BlockSpec((tm, LANES), lambda i: (i % nt_seq, 0)),
        ],
        out_specs=[
            pl.BlockSpec((H, tm, LANES), lambda i: (0, i, 0)),
            pl.BlockSpec((H, tm, LANES), lambda i: (0, i, 0)),
            pl.BlockSpec((H, tm, LANES), lambda i: (0, i, 0)),
            pl.BlockSpec((tm, 3 * RW_WIDTH), lambda i: (i, 0)),
            pl.BlockSpec((tm, RW_LORA_PAD), lambda i: (i, 0)),
        ],
        out_shape=[
            jax.ShapeDtypeStruct((H, T, LANES), BF16),
            jax.ShapeDtypeStruct((H, T, LANES), BF16),
            jax.ShapeDtypeStruct((H, T, LANES), BF16),
            jax.ShapeDtypeStruct((T, 3 * RW_WIDTH), F32),
            jax.ShapeDtypeStruct((T, RW_LORA_PAD), F32),
        ],
        compiler_params=_cparams(("parallel",)),
        name="even_proj",
    )(x, bw(wq), bw(wkv), bw(wkra), bw(wkrb), bw(wrkv), bw(wlora),
      q_norm.reshape(1, -1), kv_norm.reshape(1, -1),
      bw(wuq), bw(wuqs), bw(wuk), bw(wuv), cos_t, sin_t)
    return outs


def _attn_kernel(q_ref, k_ref, v_ref, o_ref, *, n_kv, tk):
    qa = q_ref[0]
    qb = q_ref[1]
    tq = qa.shape[0]

    def one_head(q, kk, vv, m, acc):
        s = lax.dot_general(q, kk, _NT, preferred_element_type=F32)
        m_new = jnp.maximum(m, jnp.max(s, axis=-1, keepdims=True))
        alpha = jnp.exp2(m - m_new)
        p = jnp.exp2((s - m_new).astype(BF16))
        acc = alpha * acc + jnp.dot(p, vv, preferred_element_type=F32)
        return m_new, acc

    def body(c, carry):
        ma, acca, mb, accb = carry
        rows = pl.ds(pl.multiple_of(c * tk, tk), tk)
        ma, acca = one_head(qa, k_ref[0, rows, :], v_ref[0, rows, :], ma, acca)
        mb, accb = one_head(qb, k_ref[1, rows, :], v_ref[1, rows, :], mb, accb)
        return ma, acca, mb, accb

    m0 = jnp.full((tq, 1), -jnp.inf, F32)
    a0 = jnp.zeros((tq, LANES), F32)
    _, acca, _, accb = lax.fori_loop(0, n_kv, body, (m0, a0, m0, a0), unroll=4)
    lane = lax.broadcasted_iota(jnp.int32, (tq, LANES), 1)
    oa = acca / acca[:, MLA_V:MLA_V + 1]
    ob = pltpu.roll(accb / accb[:, MLA_V:MLA_V + 1], MLA_V, axis=1)
    o_ref[...] = jnp.where(lane < MLA_V, oa, ob).astype(o_ref.dtype)


def _attention(q, k, v, B, S, *, tq_pref=1024, tk_pref=512):
    H, T, _ = q.shape
    tq = _tile(S, tq_pref)
    tk = _tile(S, tk_pref)
    nq = S // tq
    return pl.pallas_call(
        functools.partial(_attn_kernel, n_kv=S // tk, tk=tk),
        grid=(B, H // 2, nq),
        in_specs=[
            pl.BlockSpec((2, tq, LANES), lambda b, p, i: (p, b * nq + i, 0)),
            pl.BlockSpec((2, S, LANES), lambda b, p, i: (p, b, 0)),
            pl.BlockSpec((2, S, LANES), lambda b, p, i: (p, b, 0)),
        ],
        out_specs=pl.BlockSpec((tq, LANES), lambda b, p, i: (b * nq + i, p)),
        out_shape=jax.ShapeDtypeStruct((T, H * MLA_V), BF16),
        compiler_params=_cparams(("parallel", "parallel", "arbitrary")),
        name="mla_attention",
    )(q, k, v)


def _shifted(main_ref, prev_ref, next_ref, ext_ref, first, last, reach):
    tm = main_ref.shape[0]
    ext_ref[0:HALO, :] = jnp.where(first, 0.0, prev_ref[...])
    ext_ref[HALO:HALO + tm, :] = main_ref[...]
    ext_ref[HALO + tm:2 * HALO + tm, :] = jnp.where(last, 0.0, next_ref[...])
    del reach

    def load(d):
        return ext_ref[HALO + d:HALO + d + tm, :]

    return load


def _head_sum(x, ones_ref):
    return _dot_exact_rhs(x, ones_ref[...])


def _rwkv_prep_kernel(rkv_ref, rkvp_ref, rkvn_ref, lo_ref, lop_ref, lon_ref,
                      mu_rkv_ref, mu_lo_ref, wl_ref, bias_ref, vec_ref, ones_ref,
                      r_ref, k_ref, v_ref, kk_ref, kka_ref, lw_ref, bonus_ref, g_ref,
                      ext_rkv, ext_lo, *, nt_seq):
    i = pl.program_id(0)
    first = (i % nt_seq) == 0
    last = (i % nt_seq) == nt_seq - 1
    W = RW_WIDTH

    ld = _shifted(rkv_ref, rkvp_ref, rkvn_ref, ext_rkv, first, last, 1)
    cur = rkv_ref[...]
    rkv = cur + (0.5 * (ld(-1) + ld(1)) - cur) * mu_rkv_ref[...]
    ld2 = _shifted(lo_ref, lop_ref, lon_ref, ext_lo, first, last, 1)
    cur2 = lo_ref[...]
    lo = cur2 + (0.5 * (ld2(-1) + ld2(1)) - cur2) * mu_lo_ref[...]

    r = rkv[:, 0:W]
    k = rkv[:, W:2 * W]
    v = rkv[:, 2 * W:3 * W]

    lane = lax.broadcasted_iota(jnp.int32, lo.shape, 1)
    act = jnp.where(lane < 2 * DECAY_LORA, jnp.tanh(lo),
                    jnp.where(lane < 2 * DECAY_LORA + AAA_LORA, lo, jax.nn.sigmoid(lo)))
    proj = jnp.dot(act.astype(BF16), wl_ref[...], preferred_element_type=F32) + bias_ref[...]
    u_f = proj[:, 0:W]
    u_b = proj[:, W:2 * W]
    a = jax.nn.sigmoid(proj[:, 2 * W:3 * W])
    g = proj[:, 3 * W:4 * W]

    k_k = vec_ref[0:1, :]
    k_a = vec_ref[1:2, :]
    r_k = vec_ref[2:3, :]

    c = math.exp(-0.5)
    lw_ref[0] = -c * jax.nn.sigmoid(u_f)
    lw_ref[1] = -c * jax.nn.sigmoid(u_b)

    kx = k * k_k
    kk = kx * lax.rsqrt(_head_sum(kx * kx, ones_ref) + 1e-12)
    k2 = k * (1.0 + (a - 1.0) * k_a)

    r_ref[...] = r
    k_ref[...] = k2
    v_ref[...] = v
    kk_ref[...] = kk
    kka_ref[...] = kk * a
    bonus_ref[...] = _head_sum(r * k2 * r_k, ones_ref) * v
    g_ref[...] = g


def _rwkv_prep(rkv, lora, S, mu, w0, w2, a0, a2, g2, k_k, k_a, r_k, ones_blk, *, tm_pref=256):
    T = rkv.shape[0]
    tm = _tile(S, tm_pref)
    nt_seq = S // tm
    W = RW_WIDTH
    hb = tm // HALO
    last_hb = T // HALO - 1

    mu_rkv = mu[:3 * W].reshape(1, -1)
    mu_lo = jnp.pad(mu[3 * W:], (0, RW_LORA_PAD - RW_LORA)).reshape(1, -1)
    wl = jnp.zeros((RW_LORA_PAD, 4 * W), F32)
    o1, o2, o3 = DECAY_LORA, 2 * DECAY_LORA, 2 * DECAY_LORA + AAA_LORA
    wl = wl.at[0:o1, 0:W].set(w2[0])
    wl = wl.at[o1:o2, W:2 * W].set(w2[1])
    wl = wl.at[o2:o3, 2 * W:3 * W].set(a2)
    wl = wl.at[o3:o3 + GATE_LORA, 3 * W:4 * W].set(g2)
    bias = jnp.concatenate([w0[0], w0[1], a0, jnp.zeros((W,), F32)]).reshape(1, -1)
    vecs = jnp.concatenate([k_k.reshape(1, W), k_a.reshape(1, W), r_k.reshape(1, W),
                            jnp.zeros((SUBLANES - 3, W), F32)], axis=0)

    def tiled(c):
        return pl.BlockSpec((tm, c), lambda i: (i, 0))

    def prev(c):
        return pl.BlockSpec((HALO, c), lambda i: (jnp.maximum(i * hb - 1, 0), 0))

    def nxt(c):
        return pl.BlockSpec((HALO, c), lambda i: (jnp.minimum((i + 1) * hb, last_hb), 0))

    def const(shape):
        return pl.BlockSpec(shape, lambda i: (0,) * len(shape))

    o512 = jax.ShapeDtypeStruct((T, W), F32)
    return pl.pallas_call(
        functools.partial(_rwkv_prep_kernel, nt_seq=nt_seq),
        grid=(T // tm,),
        in_specs=[tiled(3 * W), prev(3 * W), nxt(3 * W),
                  tiled(RW_LORA_PAD), prev(RW_LORA_PAD), nxt(RW_LORA_PAD),
                  const((1, 3 * W)), const((1, RW_LORA_PAD)),
                  const((RW_LORA_PAD, 4 * W)), const((1, 4 * W)),
                  const((SUBLANES, W)), const((W, W))],
        out_specs=[tiled(W), tiled(W), tiled(W), tiled(W), tiled(W),
                   pl.BlockSpec((2, tm, W), lambda i: (0, i, 0)),
                   tiled(W), tiled(W)],
        out_shape=[o512, o512, o512, o512, o512,
                   jax.ShapeDtypeStruct((2, T, W), F32), o512, o512],
        scratch_shapes=[pltpu.VMEM((tm + 2 * HALO, 3 * W), F32),
                        pltpu.VMEM((tm + 2 * HALO, RW_LORA_PAD), F32)],
        compiler_params=_cparams(("parallel",)),
        name="rwkv_prep",
    )(rkv, rkv, rkv, lora, lora, lora, mu_rkv, mu_lo, wl.astype(BF16), bias, vecs, ones_blk)


def _b(a):
    return a.astype(BF16)


def _unit_tri_inverse_all(Ms, blk_masks, eye):
    n = Ms[0].shape[0]
    Md = [jnp.where(blk_masks[RW_INV_BLOCK], M, 0.0) for M in Ms]
    Mdb = [_b(M) for M in Md]
    M2 = [_dg(a, a, _NN) for a in Mdb]
    M2b = [_b(a) for a in M2]
    M4 = [_dg(a, a, _NN) for a in M2b]
    M4b = [_b(a) for a in M4]
    M8 = [_dg(a, a, _NN) for a in M4b]
    T = [_dg(_b(eye - a), _b(eye + c), _NN) for a, c in zip(Md, M2)]
    T = [_dg(_b(t), _b(eye + c), _NN) for t, c in zip(T, M4)]
    T = [_dg(_b(t), _b(eye + c), _NN) for t, c in zip(T, M8)]
    s = RW_INV_BLOCK
    while s < n:
        off = jnp.logical_and(blk_masks[2 * s], jnp.logical_not(blk_masks[s]))
        Tb = [_b(t) for t in T]
        TC = [_dg(tb, _b(jnp.where(off, M, 0.0)), _NN) for tb, M in zip(Tb, Ms)]
        T = [t - _dg(_b(tc), tb, _NN) for t, tc, tb in zip(T, TC, Tb)]
        s *= 2
    return T


def _rwkv_scan_kernel(r_ref, k_ref, v_ref, kk_ref, kka_ref, lw_ref, y_ref, s_ref, *, n_sub):
    d = pl.program_id(0)
    c = pl.program_id(2)
    L = RW_CHUNK
    N = RW_HEAD
    H = RW_HEADS
    fwd = d == 0

    @pl.when(c == 0)
    def _():
        s_ref[...] = jnp.zeros_like(s_ref)

    row = lax.broadcasted_iota(jnp.int32, (L, L), 0)
    col = lax.broadcasted_iota(jnp.int32, (L, L), 1)
    ahead = (row - col) * (1 - 2 * d)
    earlier = ahead > 0
    upto = ahead >= 0
    cum_mask = jnp.where(upto, 1.0, 0.0).astype(BF16)
    eye = jnp.where(row == col, 1.0, 0.0).astype(F32)
    blk_masks = {}
    s = RW_INV_BLOCK
    while s <= L:
        blk_masks[s] = (row // s) == (col // s)
        s *= 2

    rows_of, w_tot = [], []
    kap, Rh, Ah, Kh, Vs = [], [], [], [], []
    for sidx in range(n_sub):
        j = jnp.where(fwd, sidx, n_sub - 1 - sidx)
        rows = pl.ds(pl.multiple_of(j * L, L), L)
        rows_of.append(rows)
        lw = lw_ref[0, rows, :]
        cs = _dot_exact_lhs(cum_mask, lw)
        e_neg = jnp.exp(-cs)
        ah = kka_ref[rows, :] * e_neg
        kh = k_ref[rows, :] * e_neg
        kp = kk_ref[rows, :] * jnp.exp(cs - lw)
        rh = r_ref[rows, :] * jnp.exp(cs)
        vv = v_ref[rows, :]
        w_tot.append(jnp.exp(jnp.where(fwd, cs[L - 1:L, :], cs[0:1, :])))
        for h in range(H):
            sl = slice(h * N, (h + 1) * N)
            kap.append(kp[:, sl])
            Rh.append(rh[:, sl])
            Ah.append(ah[:, sl])
            Kh.append(kh[:, sl])
            Vs.append(vv[:, sl])

    Vb = [_b(v) for v in Vs]
    Zb = [_b(jnp.concatenate([a, k], axis=0)) for a, k in zip(Ah, Kh)]
    G = [_dg(_b(jnp.concatenate([kp, rh], axis=0)), z, _NT) for kp, rh, z in zip(kap, Rh, Zb)]
    Mak = [jnp.where(earlier, g[0:L, 0:L], 0.0) for g in G]
    Mkk = [_b(jnp.where(earlier, g[0:L, L:2 * L], 0.0)) for g in G]
    Mar = [_b(jnp.where(upto, g[L:2 * L, 0:L], 0.0)) for g in G]
    Mkr = [_b(jnp.where(upto, g[L:2 * L, L:2 * L], 0.0)) for g in G]
    Tinv = _unit_tri_inverse_all(Mak, blk_masks, eye)
    MV = [_dg(m, v, _NN) for m, v in zip(Mkk, Vb)]
    A12 = [_dg(_b(t), _b(jnp.concatenate([kp, mv], axis=1)), _NN) for t, kp, mv in zip(Tinv, kap, MV)]
    MA = [_dg(m, _b(a), _NN) for m, a in zip(Mar, A12)]
    Q1 = [_b(rh - ma[:, 0:N]) for rh, ma in zip(Rh, MA)]
    Y0 = [_dg(m, v, _NN) - ma[:, N:2 * N] for m, v, ma in zip(Mkr, Vb, MA)]
    Pm = [_b(eye - _dg(_b(a[:, 0:N].T), _b(ah), _NN)) for a, ah in zip(A12, Ah)]
    Q2 = [_dg(_b(jnp.concatenate([v, -a[:, N:2 * N]], axis=0).T),
              _b(jnp.concatenate([kh, ah], axis=0)), _NN)
          for v, a, kh, ah in zip(Vs, A12, Kh, Ah)]

    S = [s_ref[h] for h in range(H)]
    for sidx in range(n_sub):
        ys = []
        for h in range(H):
            p = sidx * H + h
            s_hi, s_lo = _split2(S[h])
            ys.append(_dg(Q1[p], s_hi, _NT) + _dg(Q1[p], s_lo, _NT) + Y0[p])
            S[h] = ((_dg(s_hi, Pm[p], _NN) + _dg(s_lo, Pm[p], _NN) + Q2[p])
                    * w_tot[sidx][:, h * N:(h + 1) * N])
        y_ref[0, rows_of[sidx], :] = jnp.concatenate(ys, axis=1)
    for h in range(H):
        s_ref[h] = S[h]


def _rwkv_scan(r, k, v, kk, kka, lw, B, S, *, tb_pref=256):
    T, W = r.shape
    tb = _tile(S, tb_pref)
    nb = S // tb

    def tmap(d, b, c):
        return (b * nb + jnp.where(d == 0, c, nb - 1 - c), 0)

    tiled = pl.BlockSpec((tb, W), tmap)
    dir_tiled = pl.BlockSpec((1, tb, W), lambda d, b, c: (d,) + tmap(d, b, c))
    return pl.pallas_call(
        functools.partial(_rwkv_scan_kernel, n_sub=tb // RW_CHUNK),
        grid=(2, B, nb),
        in_specs=[tiled, tiled, tiled, tiled, tiled, dir_tiled],
        out_specs=dir_tiled,
        out_shape=jax.ShapeDtypeStruct((2, T, W), F32),
        scratch_shapes=[pltpu.VMEM((RW_HEADS, RW_HEAD, RW_HEAD), F32)],
        compiler_params=_cparams(("parallel", "parallel", "arbitrary")),
        name="rwkv_scan",
    )(r, k, v, kk, kka, lw)


def _even_out_kernel(x_ref, oa_ref, y_ref, bonus_ref, g_ref, gn_ref, ones_ref,
                     wa_ref, wb_ref, lg_ref, lb_ref, o_ref):
    y = y_ref[0] + y_ref[1]
    inv_n = 1.0 / RW_HEAD
    mean = _head_sum(y, ones_ref) * inv_n
    dy = y - mean
    var = _head_sum(dy * dy, ones_ref) * inv_n
    yn = dy * lax.rsqrt(var + RW_GN_EPS) * gn_ref[0:1, :] + gn_ref[1:2, :]
    ob = ((yn + bonus_ref[...]) * g_ref[...]).astype(BF16)
    m = (jnp.dot(oa_ref[...], wa_ref[...], preferred_element_type=F32)
         + jnp.dot(ob, wb_ref[...], preferred_element_type=F32))
    o_ref[...] = _ln(DN_ALPHA * x_ref[...] + m, lg_ref[...], lb_ref[...])


def _even_out(x, oa, y, bonus, g, gn_g, gn_b, ones_blk, w_out, ln_g, ln_b, *, tm_pref=256):
    T = x.shape[0]
    tm = _tile(T, tm_pref)
    W = RW_WIDTH
    gn = jnp.concatenate([gn_g.reshape(1, W), gn_b.reshape(1, W), jnp.zeros((SUBLANES - 2, W), F32)], axis=0)
    wa = w_out[:MLA_HEADS * MLA_V].astype(BF16)
    wb = w_out[MLA_HEADS * MLA_V:].astype(BF16)

    def tiled(c):
        return pl.BlockSpec((tm, c), lambda i: (i, 0))

    def const(shape):
        return pl.BlockSpec(shape, lambda i: (0,) * len(shape))

    return pl.pallas_call(
        _even_out_kernel,
        grid=(T // tm,),
        in_specs=[tiled(D_MODEL), tiled(MLA_HEADS * MLA_V),
                  pl.BlockSpec((2, tm, W), lambda i: (0, i, 0)),
                  tiled(W), tiled(W), const((SUBLANES, W)), const((W, W)),
                  const((MLA_HEADS * MLA_V, D_MODEL)), const((W, D_MODEL)),
                  const((1, D_MODEL)), const((1, D_MODEL))],
        out_specs=tiled(D_MODEL),
        out_shape=jax.ShapeDtypeStruct((T, D_MODEL), F32),
        compiler_params=_cparams(("parallel",)),
        name="even_out",
    )(x, oa, y, bonus, g, gn, ones_blk, wa, wb, ln_g.reshape(1, -1), ln_b.reshape(1, -1))


def _odd_proj_kernel(x_ref, wz_ref, wxbc_ref, wdt_ref, z_ref, xbc_ref, dt_ref):
    xb = x_ref[...].astype(BF16)
    z_ref[...] = jnp.dot(xb, wz_ref[...], preferred_element_type=F32).astype(z_ref.dtype)
    xbc_ref[...] = jnp.dot(xb, wxbc_ref[...], preferred_element_type=F32)
    dt_ref[...] = jnp.dot(xb, wdt_ref[...], preferred_element_type=F32)


def _odd_proj(x, w_in, *, tm_pref=512):
    T = x.shape[0]
    tm = _tile(T, tm_pref)
    wz = w_in[:, :M_INNER].astype(BF16)
    wxbc = w_in[:, M_INNER:M_INNER + M_CONV_DIM].astype(BF16)
    wdt = jnp.pad(w_in[:, M_INNER + M_CONV_DIM:], ((0, 0), (0, LANES - 2 * M_HEADS))).astype(BF16)

    def tiled(c):
        return pl.BlockSpec((tm, c), lambda i: (i, 0))

    return pl.pallas_call(
        _odd_proj_kernel,
        grid=(T // tm,),
        in_specs=[tiled(D_MODEL),
                  _resident((D_MODEL, M_INNER), lambda i: (0, 0)),
                  _resident((D_MODEL, M_CONV_DIM), lambda i: (0, 0)),
                  _resident((D_MODEL, LANES), lambda i: (0, 0))],
        out_specs=[tiled(M_INNER), tiled(M_CONV_DIM), tiled(LANES)],
        out_shape=[jax.ShapeDtypeStruct((T, M_INNER), BF16),
                   jax.ShapeDtypeStruct((T, M_CONV_DIM), F32),
                   jax.ShapeDtypeStruct((T, LANES), F32)],
        compiler_params=_cparams(("parallel",)),
        name="odd_proj",
    )(x, wz, wxbc, wdt)


def _conv_kernel(u_ref, up_ref, un_ref, w_ref, b_ref, xs_ref, bm_ref, cm_ref, ext_ref, *, nt_seq):
    i = pl.program_id(0)
    first = (i % nt_seq) == 0
    last = (i % nt_seq) == nt_seq - 1
    ld = _shifted(u_ref, up_ref, un_ref, ext_ref, first, last, M_CONV // 2)
    acc = b_ref[...] + ld(-2) * w_ref[0:1, :]
    for j in range(1, M_CONV):
        acc = acc + ld(j - M_CONV // 2) * w_ref[j:j + 1, :]
    y = acc * jax.nn.sigmoid(acc)
    xs_ref[...] = y[:, 0:M_INNER]
    bm_ref[...] = y[:, M_INNER:M_INNER + M_GROUPS * M_STATE]
    cm_ref[...] = y[:, M_INNER + M_GROUPS * M_STATE:]


def _conv_silu(xbc, S, conv_w, conv_b, *, tm_pref=256):
    T, C = xbc.shape
    tm = _tile(S, tm_pref)
    nt_seq = S // tm
    hb = tm // HALO
    last_hb = T // HALO - 1
    wpad = jnp.concatenate([conv_w, jnp.zeros((SUBLANES - M_CONV, C), F32)], axis=0)
    GN = M_GROUPS * M_STATE
    return pl.pallas_call(
        functools.partial(_conv_kernel, nt_seq=nt_seq),
        grid=(T // tm,),
        in_specs=[pl.BlockSpec((tm, C), lambda i: (i, 0)),
                  pl.BlockSpec((HALO, C), lambda i: (jnp.maximum(i * hb - 1, 0), 0)),
                  pl.BlockSpec((HALO, C), lambda i: (jnp.minimum((i + 1) * hb, last_hb), 0)),
                  pl.BlockSpec((SUBLANES, C), lambda i: (0, 0)),
                  pl.BlockSpec((1, C), lambda i: (0, 0))],
        out_specs=[pl.BlockSpec((tm, M_INNER), lambda i: (i, 0)),
                   pl.BlockSpec((tm, GN), lambda i: (i, 0)),
                   pl.BlockSpec((tm, GN), lambda i: (i, 0))],
        out_shape=[jax.ShapeDtypeStruct((T, M_INNER), F32),
                   jax.ShapeDtypeStruct((T, GN), F32),
                   jax.ShapeDtypeStruct((T, GN), F32)],
        scratch_shapes=[pltpu.VMEM((tm + 2 * HALO, C), F32)],
        compiler_params=_cparams(("parallel",)),
        name="conv_silu",
    )(xbc, xbc, xbc, wpad, conv_b.reshape(1, -1))


def _ssd_kernel(xs_ref, bm_ref, cm_ref, dt_ref, dtb_ref, alog_ref, exp_ref, y_ref, st_ref):
    d = pl.program_id(0)
    c = pl.program_id(2)
    L = SSD_CHUNK
    P = M_HEADDIM
    GW = M_GROUP_W
    fwd = d == 0

    @pl.when(c == 0)
    def _():
        st_ref[...] = jnp.zeros_like(st_ref)

    row = lax.broadcasted_iota(jnp.int32, (L, L), 0)
    col = lax.broadcasted_iota(jnp.int32, (L, L), 1)
    upto = (row - col) * (1 - 2 * d) >= 0
    cum_mask = jnp.where(upto, 1.0, 0.0).astype(BF16)

    z = dt_ref[...] + dtb_ref[...]
    dt = jnp.maximum(z, 0.0) + jnp.log1p(jnp.exp(-jnp.abs(z)))
    a = dt * (-jnp.exp(alog_ref[...]))
    cs = _dot_exact_lhs(cum_mask, a)
    tot = jnp.where(fwd, cs[L - 1:L, :], cs[0:1, :])
    wgt = dt * jnp.exp(tot - cs)

    expand = exp_ref[0]
    cs_x = _dot2_rhs(cs, expand)
    wgt_x = _dot2_rhs(wgt, expand)
    tot_x = jnp.where(fwd, cs_x[L - 1:L, :], cs_x[0:1, :])

    xs = xs_ref[...]
    xs_b = xs.astype(BF16)
    into_state = (xs * wgt_x).astype(BF16)
    from_state = jnp.exp(cs_x)

    cs_d = jnp.where(fwd, cs, pltpu.roll(cs, LANES - M_HEADS, axis=1))
    dt_d = jnp.where(fwd, dt, pltpu.roll(dt, LANES - M_HEADS, axis=1))
    csT = cs_d.T
    dtT = dt_d.T
    lane = lax.broadcasted_iota(jnp.int32, (L, LANES), 1)
    lo_half = lane < P

    ys = []
    for g in range(M_GROUPS):
        gs = slice(g * GW, (g + 1) * GW)
        ns = slice(g * M_STATE, (g + 1) * M_STATE)
        Bg = bm_ref[:, ns]
        Cg = cm_ref[:, ns].astype(BF16)
        CB = lax.dot_general(Cg, Bg.astype(BF16), _NT, preferred_element_type=F32)
        st = st_ref[g]
        y_off = jnp.dot(Cg, st.astype(BF16), preferred_element_type=F32) * from_state[:, gs]
        y_pairs = []
        for pp in range(M_HPG // 2):
            ms = []
            for hh in (2 * pp, 2 * pp + 1):
                hidx = g * M_HPG + hh
                col_l = cs_d[:, hidx:hidx + 1]
                row_s = csT[hidx:hidx + 1, :]
                dec = jnp.exp(jnp.where(upto, col_l - row_s, -jnp.inf)) * dtT[hidx:hidx + 1, :]
                ms.append((CB * dec).astype(BF16))
            xpair = xs_b[:, g * GW + pp * LANES:g * GW + (pp + 1) * LANES]
            zero = jnp.zeros_like(xpair)
            rhs = jnp.concatenate([jnp.where(lo_half, xpair, zero), jnp.where(lo_half, zero, xpair)], axis=0)
            y_pairs.append(jnp.dot(jnp.concatenate(ms, axis=1), rhs, preferred_element_type=F32))
        ys.append(jnp.concatenate(y_pairs, axis=1) + y_off)
        st_ref[g] = st * jnp.exp(tot_x[:, gs]) + jnp.dot(Bg.T.astype(BF16), into_state[:, gs],
                                                          preferred_element_type=F32)
    y_ref[0] = jnp.concatenate(ys, axis=1).astype(y_ref.dtype)


def _ssd(xs, bm, cm, dt_raw, dt_bias, a_log, B, S):
    T = xs.shape[0]
    L = SSD_CHUNK
    nc = S // L
    GN = M_GROUPS * M_STATE

    def pad_lanes(v):
        return jnp.pad(v.reshape(1, -1), ((0, 0), (0, LANES - v.size)))

    heads = jnp.arange(M_INNER) // M_HEADDIM
    expand = jnp.stack([
        (jnp.arange(LANES)[:, None] == heads[None, :]),
        (jnp.arange(LANES)[:, None] == (heads[None, :] + M_HEADS)),
    ]).astype(BF16)

    def tmap(d, b, c):
        return (b * nc + jnp.where(d == 0, c, nc - 1 - c), 0)

    def tiled(w):
        return pl.BlockSpec((L, w), tmap)

    return pl.pallas_call(
        _ssd_kernel,
        grid=(2, B, nc),
        in_specs=[tiled(M_INNER), tiled(GN), tiled(GN), tiled(LANES),
                  pl.BlockSpec((1, LANES), lambda d, b, c: (0, 0)),
                  pl.BlockSpec((1, LANES), lambda d, b, c: (0, 0)),
                  pl.BlockSpec((1, LANES, M_INNER), lambda d, b, c: (d, 0, 0))],
        out_specs=pl.BlockSpec((1, L, M_INNER), lambda d, b, c: (d,) + tmap(d, b, c)),
        out_shape=jax.ShapeDtypeStruct((2, T, M_INNER), BF16),
        scratch_shapes=[pltpu.VMEM((M_GROUPS, M_STATE, M_GROUP_W), F32)],
        compiler_params=_cparams(("parallel", "parallel", "arbitrary")),
        name="ssd_scan",
    )(xs, bm, cm, dt_raw, pad_lanes(dt_bias), pad_lanes(a_log), expand)


def _odd_out_kernel(x_ref, y_ref, xs_ref, z_ref, dvec_ref, ng_ref, w_ref, lg_ref, lb_ref, o_ref):
    z = z_ref[...].astype(F32)
    y = ((y_ref[0].astype(F32) + y_ref[1].astype(F32) + xs_ref[...] * dvec_ref[...])
         * (z * jax.nn.sigmoid(z)))
    parts = []
    for g in range(M_GROUPS):
        yg = y[:, g * M_GROUP_W:(g + 1) * M_GROUP_W]
        parts.append(yg * lax.rsqrt(jnp.mean(yg * yg, axis=-1, keepdims=True) + RMS_EPS))
    yn = (jnp.concatenate(parts, axis=1) * ng_ref[...]).astype(BF16)
    m = jnp.dot(yn, w_ref[...], preferred_element_type=F32)
    o_ref[...] = _ln(DN_ALPHA * x_ref[...] + m, lg_ref[...], lb_ref[...])


def _odd_out(x, y, xs, z, d_skip, norm_g, w_out, ln_g, ln_b, *, tm_pref=256):
    T = x.shape[0]
    tm = _tile(T, tm_pref)
    dvec = jnp.repeat(d_skip, M_HEADDIM).reshape(1, -1)

    def tiled(c):
        return pl.BlockSpec((tm, c), lambda i: (i, 0))

    def const(shape):
        return pl.BlockSpec(shape, lambda i: (0,) * len(shape))

    return pl.pallas_call(
        _odd_out_kernel,
        grid=(T // tm,),
        in_specs=[tiled(D_MODEL), pl.BlockSpec((2, tm, M_INNER), lambda i: (0, i, 0)),
                  tiled(M_INNER), tiled(M_INNER), const((1, M_INNER)), const((1, M_INNER)),
                  const((M_INNER, D_MODEL)), const((1, D_MODEL)), const((1, D_MODEL))],
        out_specs=tiled(D_MODEL),
        out_shape=jax.ShapeDtypeStruct((T, D_MODEL), F32),
        compiler_params=_cparams(("parallel",)),
        name="odd_out",
    )(x, y, xs, z, dvec, norm_g.reshape(1, -1), w_out.astype(BF16),
      ln_g.reshape(1, -1), ln_b.reshape(1, -1))


def _even_mixer_ln(x, B, S, j, P, ln_g, ln_b, ones_blk):
    q, k, v, rkv, lora = _even_proj(x, S, P['w_in_even'][j], P['mla_q_norm'][j], P['mla_w_uq'][j],
                                    P['mla_kv_norm'][j], P['mla_w_ukv'][j])
    oa = _attention(q, k, v, B, S)
    r, k2, vv, kk, kka, lw, bonus, g = _rwkv_prep(
        rkv, lora, S, P['rw_mu'][j], P['rw_w0'][j], P['rw_w2'][j], P['rw_a0'][j], P['rw_a2'][j],
        P['rw_g2'][j], P['rw_k_k'][j], P['rw_k_a'][j], P['rw_r_k'][j], ones_blk)
    y = _rwkv_scan(r, k2, vv, kk, kka, lw, B, S)
    return _even_out(x, oa, y, bonus, g, P['rw_gn_g'][j], P['rw_gn_b'][j], ones_blk,
                     P['w_out_even'][j], ln_g, ln_b)


def _odd_mixer_ln(x, B, S, j, P, ln_g, ln_b):
    z, xbc, dt_raw = _odd_proj(x, P['w_in_odd'][j])
    xs, bm, cm = _conv_silu(xbc, S, P['m_conv_w'][j], P['m_conv_b'][j])
    y = _ssd(xs, bm, cm, dt_raw, P['m_dt_bias'][j].reshape(-1), P['m_A_log'][j].reshape(-1), B, S)
    return _odd_out(x, y, xs, z, P['m_D'][j], P['m_norm_g'][j], P['w_out_odd'][j], ln_g, ln_b)


def _trunk(x3, P):
    B, S, D = x3.shape
    x = x3.reshape(B * S, D)
    head_id = jnp.arange(RW_WIDTH) // RW_HEAD
    ones_blk = (head_id[:, None] == head_id[None, :]).astype(BF16)
    for i in range(DEPTH):
        x = _ffn_ln(x, P['ffn1_in'][i], P['ffn1_out'][i], P['ln_g'][i, 0], P['ln_b'][i, 0])
        if i % 2 == 0:
            x = _even_mixer_ln(x, B, S, i // 2, P, P['ln_g'][i, 1], P['ln_b'][i, 1], ones_blk)
        else:
            x = _odd_mixer_ln(x, B, S, i // 2, P, P['ln_g'][i, 1], P['ln_b'][i, 1])
        x = _ffn_ln(x, P['ffn2_in'][i], P['ffn2_out'][i], P['ln_g'][i, 2], P['ln_b'][i, 2])
    return x.reshape(B, S, D)


def kernel(x_prompt, x_sample, ffn1_in, ffn1_out, ffn2_in, ffn2_out, ln_g, ln_b, w_in_even, w_out_even, mla_q_norm, mla_w_uq, mla_kv_norm, mla_w_ukv, rw_mu, rw_w0, rw_w2, rw_a0, rw_a2, rw_g2, rw_k_k, rw_k_a, rw_r_k, rw_gn_g, rw_gn_b, w_in_odd, m_conv_w, m_conv_b, m_dt_bias, m_A_log, m_D, m_norm_g, w_out_odd):
    P = dict(ffn1_in=ffn1_in, ffn1_out=ffn1_out, ffn2_in=ffn2_in, ffn2_out=ffn2_out,
             ln_g=ln_g, ln_b=ln_b, w_in_even=w_in_even, w_out_even=w_out_even,
             mla_q_norm=mla_q_norm, mla_w_uq=mla_w_uq, mla_kv_norm=mla_kv_norm,
             mla_w_ukv=mla_w_ukv, rw_mu=rw_mu, rw_w0=rw_w0, rw_w2=rw_w2, rw_a0=rw_a0,
             rw_a2=rw_a2, rw_g2=rw_g2, rw_k_k=rw_k_k, rw_k_a=rw_k_a, rw_r_k=rw_r_k,
             rw_gn_g=rw_gn_g, rw_gn_b=rw_gn_b, w_in_odd=w_in_odd, m_conv_w=m_conv_w,
             m_conv_b=m_conv_b, m_dt_bias=m_dt_bias, m_A_log=m_A_log, m_D=m_D,
             m_norm_g=m_norm_g, w_out_odd=w_out_odd)
    return (_trunk(x_prompt, P), _trunk(x_sample, P))
```

```python
import functools
import math

import jax
import jax.numpy as jnp
from jax import lax
from jax.experimental import pallas as pl
from jax.experimental.pallas import tpu as pltpu

F32 = jnp.float32
BF16 = jnp.bfloat16

D_MODEL = 1024
DEPTH = 4
DN_ALPHA = (2.0 * DEPTH) ** 0.25
LN_EPS = 1e-5
RMS_EPS = 1e-6
D_FF = 2816

MLA_HEADS = 8
MLA_NOPE = 64
MLA_ROPE = 32
MLA_QK = MLA_NOPE + MLA_ROPE
MLA_V = 64
Q_LORA = 384
KV_LORA = 256
ROPE_THETA = 10000.0
MLA_SCALE = MLA_QK ** -0.5
MLA_IN = Q_LORA + KV_LORA + MLA_ROPE

RW_HEADS = 8
RW_HEAD = 64
RW_WIDTH = RW_HEADS * RW_HEAD
DECAY_LORA = 64
AAA_LORA = 64
GATE_LORA = 128
RW_GN_EPS = 64e-5
RW_LORA = 2 * DECAY_LORA + AAA_LORA + GATE_LORA
RW_LORA_PAD = 384

M_INNER = 2 * D_MODEL
M_HEADDIM = 64
M_HEADS = M_INNER // M_HEADDIM
M_GROUPS = 4
M_HPG = M_HEADS // M_GROUPS
M_STATE = 128
M_CONV = 5
SSD_CHUNK = 128
M_CONV_DIM = M_INNER + 2 * M_GROUPS * M_STATE
M_GROUP_W = M_INNER // M_GROUPS

LANES = 128
SUBLANES = 8
HALO = SUBLANES
VMEM_LIMIT = 56 * 1024 * 1024

RW_CHUNK = 64
RW_INV_BLOCK = 16


def _tile(n, pref):
    t = min(n, pref)
    while n % t:
        t -= SUBLANES
    return t


def _cparams(sem):
    return pltpu.CompilerParams(dimension_semantics=sem, vmem_limit_bytes=VMEM_LIMIT)


def _ln(y, g, b):
    mu = jnp.mean(y, axis=-1, keepdims=True)
    d = y - mu
    var = jnp.mean(d * d, axis=-1, keepdims=True)
    return d * lax.rsqrt(var + LN_EPS) * g + b


def _bdot(a, b):
    return jnp.dot(a.astype(BF16), b.astype(BF16), preferred_element_type=F32)


def _split3(a):
    hi = a.astype(BF16)
    r1 = a - hi.astype(F32)
    mid = r1.astype(BF16)
    lo = (r1 - mid.astype(F32)).astype(BF16)
    return hi, mid, lo


def _split2(a):
    hi = a.astype(BF16)
    lo = (a - hi.astype(F32)).astype(BF16)
    return hi, lo


_NN = (((1,), (0,)), ((), ()))
_NT = (((1,), (1,)), ((), ()))


def _dg(a, b, dims):
    return lax.dot_general(a, b, dims, preferred_element_type=F32)


def _dot3(a, b, dims=_NN):
    ah, al = _split2(a)
    bh, bl = _split2(b)
    return _dg(ah, bh, dims) + (_dg(ah, bl, dims) + _dg(al, bh, dims))


def _dot_exact_lhs(a_bf16, b):
    b0, b1, b2 = _split3(b)
    return _dg(a_bf16, b0, _NN) + (_dg(a_bf16, b1, _NN) + _dg(a_bf16, b2, _NN))


def _dot_exact_rhs(a, b_bf16):
    a0, a1, a2 = _split3(a)
    return _dg(a0, b_bf16, _NN) + (_dg(a1, b_bf16, _NN) + _dg(a2, b_bf16, _NN))


def _dot2_rhs(a, b_bf16):
    a0, a1 = _split2(a)
    return _dg(a0, b_bf16, _NN) + _dg(a1, b_bf16, _NN)


def _ffn_kernel(x_ref, wg_ref, wu_ref, wo_ref, g_ref, b_ref, o_ref):
    x = x_ref[...]
    xb = x.astype(BF16)
    gate = jnp.dot(xb, wg_ref[...], preferred_element_type=F32)
    up = jnp.dot(xb, wu_ref[...], preferred_element_type=F32)
    h = (gate * jax.nn.sigmoid(gate) * up).astype(BF16)
    m = jnp.dot(h, wo_ref[...], preferred_element_type=F32)
    o_ref[...] = _ln(DN_ALPHA * x + 0.5 * m, g_ref[...], b_ref[...])


def _resident(shape, index_map):
    return pl.BlockSpec(shape, index_map, pipeline_mode=pl.Buffered(1))


def _ffn_ln(x, w_in, w_out, g, b, *, tm_pref=512):
    T = x.shape[0]
    tm = _tile(T, tm_pref)
    w_in = w_in.astype(BF16)
    w_out = w_out.astype(BF16)
    return pl.pallas_call(
        _ffn_kernel,
        grid=(T // tm,),
        in_specs=[
            pl.BlockSpec((tm, D_MODEL), lambda i: (i, 0)),
            _resident((D_MODEL, D_FF), lambda i: (0, 0)),
            _resident((D_MODEL, D_FF), lambda i: (0, 1)),
            _resident((D_FF, D_MODEL), lambda i: (0, 0)),
            _resident((1, D_MODEL), lambda i: (0, 0)),
            _resident((1, D_MODEL), lambda i: (0, 0)),
        ],
        out_specs=pl.BlockSpec((tm, D_MODEL), lambda i: (i, 0)),
        out_shape=jax.ShapeDtypeStruct((T, D_MODEL), F32),
        compiler_params=_cparams(("parallel",)),
        name="ffn_ln",
    )(x, w_in, w_in, w_out, g.reshape(1, -1), b.reshape(1, -1))


def _even_proj_kernel(x_ref, wq_ref, wkv_ref, wkra_ref, wkrb_ref, wrkv_ref, wlora_ref,
                      qg_ref, kvg_ref, wuq_ref, wuqs_ref, wuk_ref, wuv_ref,
                      cos_ref, sin_ref,
                      q_ref, k_ref, v_ref, rkv_ref, lora_ref):
    xb = x_ref[...].astype(BF16)
    rkv_ref[...] = jnp.dot(xb, wrkv_ref[...], preferred_element_type=F32)
    lora_ref[...] = jnp.dot(xb, wlora_ref[...], preferred_element_type=F32)

    cq = jnp.dot(xb, wq_ref[...], preferred_element_type=F32)
    ckv = jnp.dot(xb, wkv_ref[...], preferred_element_type=F32)
    cqn = (cq * lax.rsqrt(jnp.mean(cq * cq, axis=-1, keepdims=True) + RMS_EPS) * qg_ref[...]).astype(BF16)
    ckvn = (ckv * lax.rsqrt(jnp.mean(ckv * ckv, axis=-1, keepdims=True) + RMS_EPS) * kvg_ref[...]).astype(BF16)

    cos = cos_ref[...]
    sin = sin_ref[...]
    lane = lax.broadcasted_iota(jnp.int32, cos.shape, 1)
    cos_q = jnp.where(lane < MLA_NOPE, 1.0, cos)

    kr = (jnp.dot(xb, wkra_ref[...], preferred_element_type=F32) * cos
          + jnp.dot(xb, wkrb_ref[...], preferred_element_type=F32) * sin)

    qa = jnp.dot(cqn, wuq_ref[...], preferred_element_type=F32)
    qs = jnp.dot(cqn, wuqs_ref[...], preferred_element_type=F32)
    kn = jnp.dot(ckvn, wuk_ref[...], preferred_element_type=F32)
    vv = jnp.dot(ckvn, wuv_ref[...], preferred_element_type=F32)
    q_scale = MLA_SCALE * math.log2(math.e)
    one_col = jnp.where(lane == MLA_V, 1.0, 0.0)
    for h in range(MLA_HEADS):
        sl = slice(h * LANES, (h + 1) * LANES)
        q_ref[h] = ((qa[:, sl] * cos_q + qs[:, sl] * sin) * q_scale).astype(BF16)
        k_ref[h] = (kn[:, sl] + kr).astype(BF16)
        v_ref[h] = (vv[:, sl] + one_col).astype(BF16)


def _rope_swap(w):
    half = MLA_ROPE // 2
    return jnp.concatenate([-w[..., half:], w[..., :half]], axis=-1)


def _even_proj(x, S, w_in, q_norm, w_uq, kv_norm, w_ukv, *, tm_pref=256):
    T = x.shape[0]
    tm = _tile(S, tm_pref)
    nt_seq = S // tm
    H = MLA_HEADS

    wq = w_in[:, :Q_LORA]
    wkv = w_in[:, Q_LORA:Q_LORA + KV_LORA]
    wkr = w_in[:, Q_LORA + KV_LORA:MLA_IN]
    zk = jnp.zeros((D_MODEL, MLA_NOPE), F32)
    zt = jnp.zeros((D_MODEL, LANES - MLA_QK), F32)
    wkra = jnp.concatenate([zk, wkr, zt], axis=1)
    wkrb = jnp.concatenate([zk, _rope_swap(wkr), zt], axis=1)
    wrw = w_in[:, MLA_IN:]
    wrkv = wrw[:, :3 * RW_WIDTH]
    wlora = jnp.pad(wrw[:, 3 * RW_WIDTH:], ((0, 0), (0, RW_LORA_PAD - RW_LORA)))

    uq = w_uq.reshape(Q_LORA, H, MLA_QK)
    pad_q = jnp.zeros((Q_LORA, H, LANES - MLA_QK), F32)
    wuq = jnp.concatenate([uq, pad_q], axis=-1).reshape(Q_LORA, H * LANES)
    wuqs = jnp.concatenate([jnp.zeros((Q_LORA, H, MLA_NOPE), F32), _rope_swap(uq[..., MLA_NOPE:]), pad_q],
                           axis=-1).reshape(Q_LORA, H * LANES)
    ukv = w_ukv.reshape(KV_LORA, H, MLA_NOPE + MLA_V)
    wuk = jnp.concatenate([ukv[..., :MLA_NOPE], jnp.zeros((KV_LORA, H, LANES - MLA_NOPE), F32)],
                          axis=-1).reshape(KV_LORA, H * LANES)
    wuv = jnp.concatenate([ukv[..., MLA_NOPE:], jnp.zeros((KV_LORA, H, LANES - MLA_V), F32)],
                          axis=-1).reshape(KV_LORA, H * LANES)

    half = MLA_ROPE // 2
    inv = ROPE_THETA ** (-jnp.arange(half, dtype=F32) / half)
    ang = jnp.arange(S, dtype=F32)[:, None] * inv[None, :]
    zl = jnp.zeros((S, MLA_NOPE), F32)
    zr = jnp.zeros((S, LANES - MLA_QK), F32)
    cos_t = jnp.concatenate([zl, jnp.cos(ang), jnp.cos(ang), zr], axis=1)
    sin_t = jnp.concatenate([zl, jnp.sin(ang), jnp.sin(ang), zr], axis=1)

    def const(shape):
        return pl.BlockSpec(shape, lambda i: (0,) * len(shape))

    bw = lambda a: a.astype(BF16)
    outs = pl.pallas_call(
        _even_proj_kernel,
        grid=(T // tm,),
        in_specs=[
            pl.BlockSpec((tm, D_MODEL), lambda i: (i, 0)),
            const((D_MODEL, Q_LORA)), const((D_MODEL, KV_LORA)),
            const((D_MODEL, LANES)), const((D_MODEL, LANES)),
            const((D_MODEL, 3 * RW_WIDTH)), const((D_MODEL, RW_LORA_PAD)),
            const((1, Q_LORA)), const((1, KV_LORA)),
            const((Q_LORA, H * LANES)), const((Q_LORA, H * LANES)),
            const((KV_LORA, H * LANES)), const((KV_LORA, H * LANES)),
            pl.BlockSpec((tm, LANES), lambda i: (i % nt_seq, 0)),
            pl.BlockSpec((tm, LANES), lambda i: (i % nt_seq, 0)),
        ],
        out_specs=[
            pl.BlockSpec((H, tm, LANES), lambda i: (0, i, 0)),
            pl.BlockSpec((H, tm, LANES), lambda i: (0, i, 0)),
            pl.BlockSpec((H, tm, LANES), lambda i: (0, i, 0)),
            pl.BlockSpec((tm, 3 * RW_WIDTH), lambda i: (i, 0)),
            pl.BlockSpec((tm, RW_LORA_PAD), lambda i: (i, 0)),
        ],
        out_shape=[
            jax.ShapeDtypeStruct((H, T, LANES), BF16),
            jax.ShapeDtypeStruct((H, T, LANES), BF16),
            jax.ShapeDtypeStruct((H, T, LANES), BF16),
            jax.ShapeDtypeStruct((T, 3 * RW_WIDTH), F32),
            jax.ShapeDtypeStruct((T, RW_LORA_PAD), F32),
        ],
        compiler_params=_cparams(("parallel",)),
        name="even_proj",
    )(x, bw(wq), bw(wkv), bw(wkra), bw(wkrb), bw(wrkv), bw(wlora),
      q_norm.reshape(1, -1), kv_norm.reshape(1, -1),
      bw(wuq), bw(wuqs), bw(wuk), bw(wuv), cos_t, sin_t)
    return outs


def _attn_kernel(q_ref, k_ref, v_ref, o_ref, *, n_kv, tk):
    qa = q_ref[0]
    qb = q_ref[1]
    tq = qa.shape[0]

    def one_head(q, kk, vv, m, acc):
        s = lax.dot_general(q, kk, _NT, preferred_element_type=F32)
        m_new = jnp.maximum(m, jnp.max(s, axis=-1, keepdims=True))
        alpha = jnp.exp2(m - m_new)
        p = jnp.exp2((s - m_new).astype(BF16))
        acc = alpha * acc + jnp.dot(p, vv, preferred_element_type=F32)
        return m_new, acc

    def body(c, carry):
        ma, acca, mb, accb = carry
        rows = pl.ds(pl.multiple_of(c * tk, tk), tk)
        ma, acca = one_head(qa, k_ref[0, rows, :], v_ref[0, rows, :], ma, acca)
        mb, accb = one_head(qb, k_ref[1, rows, :], v_ref[1, rows, :], mb, accb)
        return ma, acca, mb, accb

    m0 = jnp.full((tq, 1), -jnp.inf, F32)
    a0 = jnp.zeros((tq, LANES), F32)
    _, acca, _, accb = lax.fori_loop(0, n_kv, body, (m0, a0, m0, a0), unroll=4)
    lane = lax.broadcasted_iota(jnp.int32, (tq, LANES), 1)
    oa = acca / acca[:, MLA_V:MLA_V + 1]
    ob = pltpu.roll(accb / accb[:, MLA_V:MLA_V + 1], MLA_V, axis=1)
    o_ref[...] = jnp.where(lane < MLA_V, oa, ob).astype(o_ref.dtype)


def _attention(q, k, v, B, S, *, tq_pref=1024, tk_pref=512):
    H, T, _ = q.shape
    tq = _tile(S, tq_pref)
    tk = _tile(S, tk_pref)
    nq = S // tq
    return pl.pallas_call(
        functools.partial(_attn_kernel, n_kv=S // tk, tk=tk),
        grid=(B, H // 2, nq),
        in_specs=[
            pl.BlockSpec((2, tq, LANES), lambda b, p, i: (p, b * nq + i, 0)),
            pl.BlockSpec((2, S, LANES), lambda b, p, i: (p, b, 0)),
            pl.BlockSpec((2, S, LANES), lambda b, p, i: (p, b, 0)),
        ],
        out_specs=pl.BlockSpec((tq, LANES), lambda b, p, i: (b * nq + i, p)),
        out_shape=jax.ShapeDtypeStruct((T, H * MLA_V), BF16),
        compiler_params=_cparams(("parallel", "parallel", "arbitrary")),
        name="mla_attention",
    )(q, k, v)


def _shifted(main_ref, prev_ref, next_ref, ext_ref, first, last, reach):
    tm = main_ref.shape[0]
    ext_ref[0:HALO, :] = jnp.where(first, 0.0, prev_ref[...])
    ext_ref[HALO:HALO + tm, :] = main_ref[...]
    ext_ref[HALO + tm:2 * HALO + tm, :] = jnp.where(last, 0.0, next_ref[...])
    del reach

    def load(d):
        return ext_ref[HALO + d:HALO + d + tm, :]

    return load


def _head_sum(x, ones_ref):
    return _dot_exact_rhs(x, ones_ref[...])


def _rwkv_prep_kernel(rkv_ref, rkvp_ref, rkvn_ref, lo_ref, lop_ref, lon_ref,
                      mu_rkv_ref, mu_lo_ref, wl_ref, bias_ref, vec_ref, ones_ref,
                      r_ref, k_ref, v_ref, kk_ref, kka_ref, lw_ref, bonus_ref, g_ref,
                      ext_rkv, ext_lo, *, nt_seq):
    i = pl.program_id(0)
    first = (i % nt_seq) == 0
    last = (i % nt_seq) == nt_seq - 1
    W = RW_WIDTH

    ld = _shifted(rkv_ref, rkvp_ref, rkvn_ref, ext_rkv, first, last, 1)
    cur = rkv_ref[...]
    rkv = cur + (0.5 * (ld(-1) + ld(1)) - cur) * mu_rkv_ref[...]
    ld2 = _shifted(lo_ref, lop_ref, lon_ref, ext_lo, first, last, 1)
    cur2 = lo_ref[...]
    lo = cur2 + (0.5 * (ld2(-1) + ld2(1)) - cur2) * mu_lo_ref[...]

    r = rkv[:, 0:W]
    k = rkv[:, W:2 * W]
    v = rkv[:, 2 * W:3 * W]

    lane = lax.broadcasted_iota(jnp.int32, lo.shape, 1)
    act = jnp.where(lane < 2 * DECAY_LORA, jnp.tanh(lo),
                    jnp.where(lane < 2 * DECAY_LORA + AAA_LORA, lo, jax.nn.sigmoid(lo)))
    proj = jnp.dot(act.astype(BF16), wl_ref[...], preferred_element_type=F32) + bias_ref[...]
    u_f = proj[:, 0:W]
    u_b = proj[:, W:2 * W]
    a = jax.nn.sigmoid(proj[:, 2 * W:3 * W])
    g = proj[:, 3 * W:4 * W]

    k_k = vec_ref[0:1, :]
    k_a = vec_ref[1:2, :]
    r_k = vec_ref[2:3, :]

    c = math.exp(-0.5)
    lw_ref[0] = -c * jax.nn.sigmoid(u_f)
    lw_ref[1] = -c * jax.nn.sigmoid(u_b)

    kx = k * k_k
    kk = kx * lax.rsqrt(_head_sum(kx * kx, ones_ref) + 1e-12)
    k2 = k * (1.0 + (a - 1.0) * k_a)

    r_ref[...] = r
    k_ref[...] = k2
    v_ref[...] = v
    kk_ref[...] = kk
    kka_ref[...] = kk * a
    bonus_ref[...] = _head_sum(r * k2 * r_k, ones_ref) * v
    g_ref[...] = g


def _rwkv_prep(rkv, lora, S, mu, w0, w2, a0, a2, g2, k_k, k_a, r_k, ones_blk, *, tm_pref=256):
    T = rkv.shape[0]
    tm = _tile(S, tm_pref)
    nt_seq = S // tm
    W = RW_WIDTH
    hb = tm // HALO
    last_hb = T // HALO - 1

    mu_rkv = mu[:3 * W].reshape(1, -1)
    mu_lo = jnp.pad(mu[3 * W:], (0, RW_LORA_PAD - RW_LORA)).reshape(1, -1)
    wl = jnp.zeros((RW_LORA_PAD, 4 * W), F32)
    o1, o2, o3 = DECAY_LORA, 2 * DECAY_LORA, 2 * DECAY_LORA + AAA_LORA
    wl = wl.at[0:o1, 0:W].set(w2[0])
    wl = wl.at[o1:o2, W:2 * W].set(w2[1])
    wl = wl.at[o2:o3, 2 * W:3 * W].set(a2)
    wl = wl.at[o3:o3 + GATE_LORA, 3 * W:4 * W].set(g2)
    bias = jnp.concatenate([w0[0], w0[1], a0, jnp.zeros((W,), F32)]).reshape(1, -1)
    vecs = jnp.concatenate([k_k.reshape(1, W), k_a.reshape(1, W), r_k.reshape(1, W),
                            jnp.zeros((SUBLANES - 3, W), F32)], axis=0)

    def tiled(c):
        return pl.BlockSpec((tm, c), lambda i: (i, 0))

    def prev(c):
        return pl.BlockSpec((HALO, c), lambda i: (jnp.maximum(i * hb - 1, 0), 0))

    def nxt(c):
        return pl.BlockSpec((HALO, c), lambda i: (jnp.minimum((i + 1) * hb, last_hb), 0))

    def const(shape):
        return pl.BlockSpec(shape, lambda i: (0,) * len(shape))

    o512 = jax.ShapeDtypeStruct((T, W), F32)
    return pl.pallas_call(
        functools.partial(_rwkv_prep_kernel, nt_seq=nt_seq),
        grid=(T // tm,),
        in_specs=[tiled(3 * W), prev(3 * W), nxt(3 * W),
                  tiled(RW_LORA_PAD), prev(RW_LORA_PAD), nxt(RW_LORA_PAD),
                  const((1, 3 * W)), const((1, RW_LORA_PAD)),
                  const((RW_LORA_PAD, 4 * W)), const((1, 4 * W)),
                  const((SUBLANES, W)), const((W, W))],
        out_specs=[tiled(W), tiled(W), tiled(W), tiled(W), tiled(W),
                   pl.BlockSpec((2, tm, W), lambda i: (0, i, 0)),
                   tiled(W), tiled(W)],
        out_shape=[o512, o512, o512, o512, o512,
                   jax.ShapeDtypeStruct((2, T, W), F32), o512, o512],
        scratch_shapes=[pltpu.VMEM((tm + 2 * HALO, 3 * W), F32),
                        pltpu.VMEM((tm + 2 * HALO, RW_LORA_PAD), F32)],
        compiler_params=_cparams(("parallel",)),
        name="rwkv_prep",
    )(rkv, rkv, rkv, lora, lora, lora, mu_rkv, mu_lo, wl.astype(BF16), bias, vecs, ones_blk)


def _b(a):
    return a.astype(BF16)


def _unit_tri_inverse_all(Ms, blk_masks, eye):
    n = Ms[0].shape[0]
    Md = [jnp.where(blk_masks[RW_INV_BLOCK], M, 0.0) for M in Ms]
    Mdb = [_b(M) for M in Md]
    M2 = [_dg(a, a, _NN) for a in Mdb]
    M2b = [_b(a) for a in M2]
    M4 = [_dg(a, a, _NN) for a in M2b]
    M4b = [_b(a) for a in M4]
    M8 = [_dg(a, a, _NN) for a in M4b]
    T = [_dg(_b(eye - a), _b(eye + c), _NN) for a, c in zip(Md, M2)]
    T = [_dg(_b(t), _b(eye + c), _NN) for t, c in zip(T, M4)]
    T = [_dg(_b(t), _b(eye + c), _NN) for t, c in zip(T, M8)]
    s = RW_INV_BLOCK
    while s < n:
        off = jnp.logical_and(blk_masks[2 * s], jnp.logical_not(blk_masks[s]))
        Tb = [_b(t) for t in T]
        TC = [_dg(tb, _b(jnp.where(off, M, 0.0)), _NN) for tb, M in zip(Tb, Ms)]
        T = [t - _dg(_b(tc), tb, _NN) for t, tc, tb in zip(T, TC, Tb)]
        s *= 2
    return T


def _rwkv_scan_kernel(r_ref, k_ref, v_ref, kk_ref, kka_ref, lw_ref, y_ref, s_ref, *, n_sub):
    d = pl.program_id(0)
    c = pl.program_id(2)
    L = RW_CHUNK
    N = RW_HEAD
    H = RW_HEADS
    fwd = d == 0

    @pl.when(c == 0)
    def _():
        s_ref[...] = jnp.zeros_like(s_ref)

    row = lax.broadcasted_iota(jnp.int32, (L, L), 0)
    col = lax.broadcasted_iota(jnp.int32, (L, L), 1)
    ahead = (row - col) * (1 - 2 * d)
    earlier = ahead > 0
    upto = ahead >= 0
    cum_mask = jnp.where(upto, 1.0, 0.0).astype(BF16)
    eye = jnp.where(row == col, 1.0, 0.0).astype(F32)
    blk_masks = {}
    s = RW_INV_BLOCK
    while s <= L:
        blk_masks[s] = (row // s) == (col // s)
        s *= 2

    rows_of, w_tot = [], []
    kap, Rh, Ah, Kh, Vs = [], [], [], [], []
    for sidx in range(n_sub):
        j = jnp.where(fwd, sidx, n_sub - 1 - sidx)
        rows = pl.ds(pl.multiple_of(j * L, L), L)
        rows_of.append(rows)
        lw = lw_ref[0, rows, :]
        cs = _dot_exact_lhs(cum_mask, lw)
        e_neg = jnp.exp(-cs)
        ah = kka_ref[rows, :] * e_neg
        kh = k_ref[rows, :] * e_neg
        kp = kk_ref[rows, :] * jnp.exp(cs - lw)
        rh = r_ref[rows, :] * jnp.exp(cs)
        vv = v_ref[rows, :]
        w_tot.append(jnp.exp(jnp.where(fwd, cs[L - 1:L, :], cs[0:1, :])))
        for h in range(H):
            sl = slice(h * N, (h + 1) * N)
            kap.append(kp[:, sl])
            Rh.append(rh[:, sl])
            Ah.append(ah[:, sl])
            Kh.append(kh[:, sl])
            Vs.append(vv[:, sl])

    Vb = [_b(v) for v in Vs]
    Zb = [_b(jnp.concatenate([a, k], axis=0)) for a, k in zip(Ah, Kh)]
    G = [_dg(_b(jnp.concatenate([kp, rh], axis=0)), z, _NT) for kp, rh, z in zip(kap, Rh, Zb)]
    Mak = [jnp.where(earlier, g[0:L, 0:L], 0.0) for g in G]
    Mkk = [_b(jnp.where(earlier, g[0:L, L:2 * L], 0.0)) for g in G]
    Mar = [_b(jnp.where(upto, g[L:2 * L, 0:L], 0.0)) for g in G]
    Mkr = [_b(jnp.where(upto, g[L:2 * L, L:2 * L], 0.0)) for g in G]
    Tinv = _unit_tri_inverse_all(Mak, blk_masks, eye)
    MV = [_dg(m, v, _NN) for m, v in zip(Mkk, Vb)]
    A12 = [_dg(_b(t), _b(jnp.concatenate([kp, mv], axis=1)), _NN) for t, kp, mv in zip(Tinv, kap, MV)]
    MA = [_dg(m, _b(a), _NN) for m, a in zip(Mar, A12)]
    Q1 = [_b(rh - ma[:, 0:N]) for rh, ma in zip(Rh, MA)]
    Y0 = [_dg(m, v, _NN) - ma[:, N:2 * N] for m, v, ma in zip(Mkr, Vb, MA)]
    Pm = [_b(eye - _dg(_b(a[:, 0:N].T), _b(ah), _NN)) for a, ah in zip(A12, Ah)]
    Q2 = [_dg(_b(jnp.concatenate([v, -a[:, N:2 * N]], axis=0).T),
              _b(jnp.concatenate([kh, ah], axis=0)), _NN)
          for v, a, kh, ah in zip(Vs, A12, Kh, Ah)]

    S = [s_ref[h] for h in range(H)]
    for sidx in range(n_sub):
        ys = []
        for h in range(H):
            p = sidx * H + h
            s_hi, s_lo = _split2(S[h])
            ys.append(_dg(Q1[p], s_hi, _NT) + _dg(Q1[p], s_lo, _NT) + Y0[p])
            S[h] = ((_dg(s_hi, Pm[p], _NN) + _dg(s_lo, Pm[p], _NN) + Q2[p])
                    * w_tot[sidx][:, h * N:(h + 1) * N])
        y_ref[0, rows_of[sidx], :] = jnp.concatenate(ys, axis=1)
    for h in range(H):
        s_ref[h] = S[h]


def _rwkv_scan(r, k, v, kk, kka, lw, B, S, *, tb_pref=256):
    T, W = r.shape
    tb = _tile(S, tb_pref)
    nb = S // tb

    def tmap(d, b, c):
        return (b * nb + jnp.where(d == 0, c, nb - 1 - c), 0)

    tiled = pl.BlockSpec((tb, W), tmap)
    dir_tiled = pl.BlockSpec((1, tb, W), lambda d, b, c: (d,) + tmap(d, b, c))
    return pl.pallas_call(
        functools.partial(_rwkv_scan_kernel, n_sub=tb // RW_CHUNK),
        grid=(2, B, nb),
        in_specs=[tiled, tiled, tiled, tiled, tiled, dir_tiled],
        out_specs=dir_tiled,
        out_shape=jax.ShapeDtypeStruct((2, T, W), F32),
        scratch_shapes=[pltpu.VMEM((RW_HEADS, RW_HEAD, RW_HEAD), F32)],
        compiler_params=_cparams(("parallel", "parallel", "arbitrary")),
        name="rwkv_scan",
    )(r, k, v, kk, kka, lw)


def _even_out_kernel(x_ref, oa_ref, y_ref, bonus_ref, g_ref, gn_ref, ones_ref,
                     wa_ref, wb_ref, lg_ref, lb_ref, o_ref):
    y = y_ref[0] + y_ref[1]
    inv_n = 1.0 / RW_HEAD
    mean = _head_sum(y, ones_ref) * inv_n
    dy = y - mean
    var = _head_sum(dy * dy, ones_ref) * inv_n
    yn = dy * lax.rsqrt(var + RW_GN_EPS) * gn_ref[0:1, :] + gn_ref[1:2, :]
    ob = ((yn + bonus_ref[...]) * g_ref[...]).astype(BF16)
    m = (jnp.dot(oa_ref[...], wa_ref[...], preferred_element_type=F32)
         + jnp.dot(ob, wb_ref[...], preferred_element_type=F32))
    o_ref[...] = _ln(DN_ALPHA * x_ref[...] + m, lg_ref[...], lb_ref[...])


def _even_out(x, oa, y, bonus, g, gn_g, gn_b, ones_blk, w_out, ln_g, ln_b, *, tm_pref=256):
    T = x.shape[0]
    tm = _tile(T, tm_pref)
    W = RW_WIDTH
    gn = jnp.concatenate([gn_g.reshape(1, W), gn_b.reshape(1, W), jnp.zeros((SUBLANES - 2, W), F32)], axis=0)
    wa = w_out[:MLA_HEADS * MLA_V].astype(BF16)
    wb = w_out[MLA_HEADS * MLA_V:].astype(BF16)

    def tiled(c):
        return pl.BlockSpec((tm, c), lambda i: (i, 0))

    def const(shape):
        return pl.BlockSpec(shape, lambda i: (0,) * len(shape))

    return pl.pallas_call(
        _even_out_kernel,
        grid=(T // tm,),
        in_specs=[tiled(D_MODEL), tiled(MLA_HEADS * MLA_V),
                  pl.BlockSpec((2, tm, W), lambda i: (0, i, 0)),
                  tiled(W), tiled(W), const((SUBLANES, W)), const((W, W)),
                  const((MLA_HEADS * MLA_V, D_MODEL)), const((W, D_MODEL)),
                  const((1, D_MODEL)), const((1, D_MODEL))],
        out_specs=tiled(D_MODEL),
        out_shape=jax.ShapeDtypeStruct((T, D_MODEL), F32),
        compiler_params=_cparams(("parallel",)),
        name="even_out",
    )(x, oa, y, bonus, g, gn, ones_blk, wa, wb, ln_g.reshape(1, -1), ln_b.reshape(1, -1))


def _odd_proj_kernel(x_ref, wz_ref, wxbc_ref, wdt_ref, z_ref, xbc_ref, dt_ref):
    xb = x_ref[...].astype(BF16)
    z_ref[...] = jnp.dot(xb, wz_ref[...], preferred_element_type=F32).astype(z_ref.dtype)
    xbc_ref[...] = jnp.dot(xb, wxbc_ref[...], preferred_element_type=F32)
    dt_ref[...] = jnp.dot(xb, wdt_ref[...], preferred_element_type=F32)


def _odd_proj(x, w_in, *, tm_pref=512):
    T = x.shape[0]
    tm = _tile(T, tm_pref)
    wz = w_in[:, :M_INNER].astype(BF16)
    wxbc = w_in[:, M_INNER:M_INNER + M_CONV_DIM].astype(BF16)
    wdt = jnp.pad(w_in[:, M_INNER + M_CONV_DIM:], ((0, 0), (0, LANES - 2 * M_HEADS))).astype(BF16)

    def tiled(c):
        return pl.BlockSpec((tm, c), lambda i: (i, 0))

    return pl.pallas_call(
        _odd_proj_kernel,
        grid=(T // tm,),
        in_specs=[tiled(D_MODEL),
                  _resident((D_MODEL, M_INNER), lambda i: (0, 0)),
                  _resident((D_MODEL, M_CONV_DIM), lambda i: (0, 0)),
                  _resident((D_MODEL, LANES), lambda i: (0, 0))],
        out_specs=[tiled(M_INNER), tiled(M_CONV_DIM), tiled(LANES)],
        out_shape=[jax.ShapeDtypeStruct((T, M_INNER), BF16),
                   jax.ShapeDtypeStruct((T, M_CONV_DIM), F32),
                   jax.ShapeDtypeStruct((T, LANES), F32)],
        compiler_params=_cparams(("parallel",)),
        name="odd_proj",
    )(x, wz, wxbc, wdt)


def _conv_kernel(u_ref, up_ref, un_ref, w_ref, b_ref, xs_ref, bm_ref, cm_ref, ext_ref, *, nt_seq):
    i = pl.program_id(0)
    first = (i % nt_seq) == 0
    last = (i % nt_seq) == nt_seq - 1
    ld = _shifted(u_ref, up_ref, un_ref, ext_ref, first, last, M_CONV // 2)
    acc = b_ref[...] + ld(-2) * w_ref[0:1, :]
    for j in range(1, M_CONV):
        acc = acc + ld(j - M_CONV // 2) * w_ref[j:j + 1, :]
    y = acc * jax.nn.sigmoid(acc)
    xs_ref[...] = y[:, 0:M_INNER]
    bm_ref[...] = y[:, M_INNER:M_INNER + M_GROUPS * M_STATE]
    cm_ref[...] = y[:, M_INNER + M_GROUPS * M_STATE:]


def _conv_silu(xbc, S, conv_w, conv_b, *, tm_pref=256):
    T, C = xbc.shape
    tm = _tile(S, tm_pref)
    nt_seq = S // tm
    hb = tm // HALO
    last_hb = T // HALO - 1
    wpad = jnp.concatenate([conv_w, jnp.zeros((SUBLANES - M_CONV, C), F32)], axis=0)
    GN = M_GROUPS * M_STATE
    return pl.pallas_call(
        functools.partial(_conv_kernel, nt_seq=nt_seq),
        grid=(T // tm,),
        in_specs=[pl.BlockSpec((tm, C), lambda i: (i, 0)),
                  pl.BlockSpec((HALO, C), lambda i: (jnp.maximum(i * hb - 1, 0), 0)),
                  pl.BlockSpec((HALO, C), lambda i: (jnp.minimum((i + 1) * hb, last_hb), 0)),
                  pl.BlockSpec((SUBLANES, C), lambda i: (0, 0)),
                  pl.BlockSpec((1, C), lambda i: (0, 0))],
        out_specs=[pl.BlockSpec((tm, M_INNER), lambda i: (i, 0)),
                   pl.BlockSpec((tm, GN), lambda i: (i, 0)),
                   pl.BlockSpec((tm, GN), lambda i: (i, 0))],
        out_shape=[jax.ShapeDtypeStruct((T, M_INNER), F32),
                   jax.ShapeDtypeStruct((T, GN), F32),
                   jax.ShapeDtypeStruct((T, GN), F32)],
        scratch_shapes=[pltpu.VMEM((tm + 2 * HALO, C), F32)],
        compiler_params=_cparams(("parallel",)),
        name="conv_silu",
    )(xbc, xbc, xbc, wpad, conv_b.reshape(1, -1))


def _ssd_kernel(xs_ref, bm_ref, cm_ref, dt_ref, dtb_ref, alog_ref, exp_ref, y_ref, st_ref):
    d = pl.program_id(0)
    c = pl.program_id(2)
    L = SSD_CHUNK
    P = M_HEADDIM
    GW = M_GROUP_W
    fwd = d == 0

    @pl.when(c == 0)
    def _():
        st_ref[...] = jnp.zeros_like(st_ref)

    row = lax.broadcasted_iota(jnp.int32, (L, L), 0)
    col = lax.broadcasted_iota(jnp.int32, (L, L), 1)
    upto = (row - col) * (1 - 2 * d) >= 0
    cum_mask = jnp.where(upto, 1.0, 0.0).astype(BF16)

    z = dt_ref[...] + dtb_ref[...]
    dt = jnp.maximum(z, 0.0) + jnp.log1p(jnp.exp(-jnp.abs(z)))
    a = dt * (-jnp.exp(alog_ref[...]))
    cs = _dot_exact_lhs(cum_mask, a)
    tot = jnp.where(fwd, cs[L - 1:L, :], cs[0:1, :])
    wgt = dt * jnp.exp(tot - cs)

    cs_hi, cs_lo = _split2(cs)
    wg_hi, wg_lo = _split2(wgt)

    cs_d = jnp.where(fwd, cs, pltpu.roll(cs, LANES - M_HEADS, axis=1))
    dt_d = jnp.where(fwd, dt, pltpu.roll(dt, LANES - M_HEADS, axis=1))
    csT = cs_d.T
    dtT = dt_d.T
    lane = lax.broadcasted_iota(jnp.int32, (L, LANES), 1)
    lo_half = lane < P

    ys = []
    for g in range(M_GROUPS):
        gs = slice(g * GW, (g + 1) * GW)
        ns = slice(g * M_STATE, (g + 1) * M_STATE)
        Bg = bm_ref[:, ns]
        Cg = cm_ref[:, ns].astype(BF16)
        CB = lax.dot_general(Cg, Bg.astype(BF16), _NT, preferred_element_type=F32)
        st = st_ref[g]
        ex_g = exp_ref[0, :, gs]
        cs_xg = _dg(cs_hi, ex_g, _NN) + _dg(cs_lo, ex_g, _NN)
        wgt_xg = _dg(wg_hi, ex_g, _NN) + _dg(wg_lo, ex_g, _NN)
        tot_xg = jnp.where(fwd, cs_xg[L - 1:L, :], cs_xg[0:1, :])
        xs_g = xs_ref[:, gs]
        xs_bg = xs_g.astype(BF16)
        into_g = (xs_g * wgt_xg).astype(BF16)
        y_off = jnp.dot(Cg, st.astype(BF16), preferred_element_type=F32) * jnp.exp(cs_xg)
        y_pairs = []
        for pp in range(M_HPG // 2):
            ms = []
            for hh in (2 * pp, 2 * pp + 1):
                hidx = g * M_HPG + hh
                col_l = cs_d[:, hidx:hidx + 1]
                row_s = csT[hidx:hidx + 1, :]
                dec = jnp.exp(jnp.where(upto, col_l - row_s, -jnp.inf)) * dtT[hidx:hidx + 1, :]
                ms.append((CB * dec).astype(BF16))
            xpair = xs_bg[:, pp * LANES:(pp + 1) * LANES]
            zero = jnp.zeros_like(xpair)
            rhs = jnp.concatenate([jnp.where(lo_half, xpair, zero), jnp.where(lo_half, zero, xpair)], axis=0)
            y_pairs.append(jnp.dot(jnp.concatenate(ms, axis=1), rhs, preferred_element_type=F32))
        ys.append(jnp.concatenate(y_pairs, axis=1) + y_off)
        st_ref[g] = st * jnp.exp(tot_xg) + jnp.dot(Bg.T.astype(BF16), into_g,
                                                    preferred_element_type=F32)
    y_ref[0] = jnp.concatenate(ys, axis=1).astype(y_ref.dtype)


def _ssd(xs, bm, cm, dt_raw, dt_bias, a_log, B, S):
    T = xs.shape[0]
    L = SSD_CHUNK
    nc = S // L
    GN = M_GROUPS * M_STATE

    def pad_lanes(v):
        return jnp.pad(v.reshape(1, -1), ((0, 0), (0, LANES - v.size)))

    heads = jnp.arange(M_INNER) // M_HEADDIM
    expand = jnp.stack([
        (jnp.arange(LANES)[:, None] == heads[None, :]),
        (jnp.arange(LANES)[:, None] == (heads[None, :] + M_HEADS)),
    ]).astype(BF16)

    def tmap(d, b, c):
        return (b * nc + jnp.where(d == 0, c, nc - 1 - c), 0)

    def tiled(w):
        return pl.BlockSpec((L, w), tmap)

    return pl.pallas_call(
        _ssd_kernel,
        grid=(2, B, nc),
        in_specs=[tiled(M_INNER), tiled(GN), tiled(GN), tiled(LANES),
                  pl.BlockSpec((1, LANES), lambda d, b, c: (0, 0)),
                  pl.BlockSpec((1, LANES), lambda d, b, c: (0, 0)),
                  pl.BlockSpec((1, LANES, M_INNER), lambda d, b, c: (d, 0, 0))],
        out_specs=pl.BlockSpec((1, L, M_INNER), lambda d, b, c: (d,) + tmap(d, b, c)),
        out_shape=jax.ShapeDtypeStruct((2, T, M_INNER), BF16),
        scratch_shapes=[pltpu.VMEM((M_GROUPS, M_STATE, M_GROUP_W), F32)],
        compiler_params=_cparams(("parallel", "parallel", "arbitrary")),
        name="ssd_scan",
    )(xs, bm, cm, dt_raw, pad_lanes(dt_bias), pad_lanes(a_log), expand)


def _odd_out_kernel(x_ref, y_ref, xs_ref, z_ref, dvec_ref, ng_ref, w_ref, lg_ref, lb_ref, o_ref):
    z = z_ref[...].astype(F32)
    y = ((y_ref[0].astype(F32) + y_ref[1].astype(F32) + xs_ref[...] * dvec_ref[...])
         * (z * jax.nn.sigmoid(z)))
    parts = []
    for g in range(M_GROUPS):
        yg = y[:, g * M_GROUP_W:(g + 1) * M_GROUP_W]
        parts.append(yg * lax.rsqrt(jnp.mean(yg * yg, axis=-1, keepdims=True) + RMS_EPS))
    yn = (jnp.concatenate(parts, axis=1) * ng_ref[...]).astype(BF16)
    m = jnp.dot(yn, w_ref[...], preferred_element_type=F32)
    o_ref[...] = _ln(DN_ALPHA * x_ref[...] + m, lg_ref[...], lb_ref[...])


def _odd_out(x, y, xs, z, d_skip, norm_g, w_out, ln_g, ln_b, *, tm_pref=256):
    T = x.shape[0]
    tm = _tile(T, tm_pref)
    dvec = jnp.repeat(d_skip, M_HEADDIM).reshape(1, -1)

    def tiled(c):
        return pl.BlockSpec((tm, c), lambda i: (i, 0))

    def const(shape):
        return pl.BlockSpec(shape, lambda i: (0,) * len(shape))

    return pl.pallas_call(
        _odd_out_kernel,
        grid=(T // tm,),
        in_specs=[tiled(D_MODEL), pl.BlockSpec((2, tm, M_INNER), lambda i: (0, i, 0)),
                  tiled(M_INNER), tiled(M_INNER), const((1, M_INNER)), const((1, M_INNER)),
                  const((M_INNER, D_MODEL)), const((1, D_MODEL)), const((1, D_MODEL))],
        out_specs=tiled(D_MODEL),
        out_shape=jax.ShapeDtypeStruct((T, D_MODEL), F32),
        compiler_params=_cparams(("parallel",)),
        name="odd_out",
    )(x, y, xs, z, dvec, norm_g.reshape(1, -1), w_out.astype(BF16),
      ln_g.reshape(1, -1), ln_b.reshape(1, -1))


def _even_mixer_ln(x, B, S, j, P, ln_g, ln_b, ones_blk):
    q, k, v, rkv, lora = _even_proj(x, S, P['w_in_even'][j], P['mla_q_norm'][j], P['mla_w_uq'][j],
                                    P['mla_kv_norm'][j], P['mla_w_ukv'][j])
    oa = _attention(q, k, v, B, S)
    r, k2, vv, kk, kka, lw, bonus, g = _rwkv_prep(
        rkv, lora, S, P['rw_mu'][j], P['rw_w0'][j], P['rw_w2'][j], P['rw_a0'][j], P['rw_a2'][j],
        P['rw_g2'][j], P['rw_k_k'][j], P['rw_k_a'][j], P['rw_r_k'][j], ones_blk)
    y = _rwkv_scan(r, k2, vv, kk, kka, lw, B, S)
    return _even_out(x, oa, y, bonus, g, P['rw_gn_g'][j], P['rw_gn_b'][j], ones_blk,
                     P['w_out_even'][j], ln_g, ln_b)


def _odd_mixer_ln(x, B, S, j, P, ln_g, ln_b):
    z, xbc, dt_raw = _odd_proj(x, P['w_in_odd'][j])
    xs, bm, cm = _conv_silu(xbc, S, P['m_conv_w'][j], P['m_conv_b'][j])
    y = _ssd(xs, bm, cm, dt_raw, P['m_dt_bias'][j].reshape(-1), P['m_A_log'][j].reshape(-1), B, S)
    return _odd_out(x, y, xs, z, P['m_D'][j], P['m_norm_g'][j], P['w_out_odd'][j], ln_g, ln_b)


def _trunk(x3, P):
    B, S, D = x3.shape
    x = x3.reshape(B * S, D)
    head_id = jnp.arange(RW_WIDTH) // RW_HEAD
    ones_blk = (head_id[:, None] == head_id[None, :]).astype(BF16)
    for i in range(DEPTH):
        x = _ffn_ln(x, P['ffn1_in'][i], P['ffn1_out'][i], P['ln_g'][i, 0], P['ln_b'][i, 0])
        if i % 2 == 0:
            x = _even_mixer_ln(x, B, S, i // 2, P, P['ln_g'][i, 1], P['ln_b'][i, 1], ones_blk)
        else:
            x = _odd_mixer_ln(x, B, S, i // 2, P, P['ln_g'][i, 1], P['ln_b'][i, 1])
        x = _ffn_ln(x, P['ffn2_in'][i], P['ffn2_out'][i], P['ln_g'][i, 2], P['ln_b'][i, 2])
    return x.reshape(B, S, D)


def kernel(x_prompt, x_sample, ffn1_in, ffn1_out, ffn2_in, ffn2_out, ln_g, ln_b, w_in_even, w_out_even, mla_q_norm, mla_w_uq, mla_kv_norm, mla_w_ukv, rw_mu, rw_w0, rw_w2, rw_a0, rw_a2, rw_g2, rw_k_k, rw_k_a, rw_r_k, rw_gn_g, rw_gn_b, w_in_odd, m_conv_w, m_conv_b, m_dt_bias, m_A_log, m_D, m_norm_g, w_out_odd):
    P = dict(ffn1_in=ffn1_in, ffn1_out=ffn1_out, ffn2_in=ffn2_in, ffn2_out=ffn2_out,
             ln_g=ln_g, ln_b=ln_b, w_in_even=w_in_even, w_out_even=w_out_even,
             mla_q_norm=mla_q_norm, mla_w_uq=mla_w_uq, mla_kv_norm=mla_kv_norm,
             mla_w_ukv=mla_w_ukv, rw_mu=rw_mu, rw_w0=rw_w0, rw_w2=rw_w2, rw_a0=rw_a0,
             rw_a2=rw_a2, rw_g2=rw_g2, rw_k_k=rw_k_k, rw_k_a=rw_k_a, rw_r_k=rw_r_k,
             rw_gn_g=rw_gn_g, rw_gn_b=rw_gn_b, w_in_odd=w_in_odd, m_conv_w=m_conv_w,
             m_conv_b=m_conv_b, m_dt_bias=m_dt_bias, m_A_log=m_A_log, m_D=m_D,
             m_norm_g=m_norm_g, w_out_odd=w_out_odd)
    return (_trunk(x_prompt, P), _trunk(x_sample, P))
```
